```python
import jax, jax.numpy as jnp
from jax import lax
import numpy as np

D_MODEL = 1024
BATCH = 8
SEQ = 4096
DEPTH = 2

N_MIXERS = 2
N_SB = (DEPTH + 1) // 2
N_MLA = DEPTH // 2
HEAD_DIM = 64
N_SB_HEADS = 12
N_MLA_HEADS = 12
MLA_Q_RANK = 768
MLA_KV_RANK = 256
MLA_NOPE = 64
MLA_ROPE = 32
MLA_QK = MLA_NOPE + MLA_ROPE
MLA_V = 64
N_MEM = 256
N_MEM_HEADS = 4
MEM_HEAD_DIM = 64
MEM_Q = N_MEM_HEADS * MEM_HEAD_DIM
MIX_WIDTH = N_SB_HEADS * HEAD_DIM + MEM_Q
SB_IN = 3 * N_SB_HEADS * HEAD_DIM + MEM_Q
MLA_IN = MLA_Q_RANK + MLA_KV_RANK + MLA_ROPE + MEM_Q
D_FF = -(-8 * D_MODEL // (3 * 256)) * 256
Q_BLOCK = 128
ROPE_THETA = 10000.0
EPS = 1e-6

kernel_name = "hybrid_stickbreak_mla_memory_swiglu"


def _rms(x, g):
    x32 = x.astype(jnp.float32)
    y = x32 * lax.rsqrt(jnp.mean(x32 * x32, axis=-1, keepdims=True) + EPS)
    return (y * g.astype(jnp.float32)).astype(x.dtype)


def _rope(x, positions):
    half = x.shape[-1] // 2
    inv_freq = ROPE_THETA ** (-jnp.arange(half, dtype=jnp.float32) / half)
    ang = positions.astype(jnp.float32)[:, :, None, None] * inv_freq
    cos, sin = jnp.cos(ang), jnp.sin(ang)
    x32 = x.astype(jnp.float32)
    x1, x2 = x32[..., :half], x32[..., half:]
    return jnp.concatenate([x1 * cos - x2 * sin, x2 * cos + x1 * sin], axis=-1).astype(x.dtype)


def _sweep_query_blocks(block_fn, q):
    b, h, s, d = q.shape
    nb = s // Q_BLOCK
    qb = jnp.moveaxis(q.reshape(b, h, nb, Q_BLOCK, d), 2, 0)
    out = lax.map(lambda a: block_fn(a[0], a[1]), (qb, jnp.arange(nb)))
    return jnp.moveaxis(out, 0, 2).reshape(b, h, s, out.shape[-1])


def _stick_breaking(q, k, v):
    s_len = k.shape[2]
    scale = q.shape[-1] ** -0.5
    key_pos = jnp.arange(s_len)[None, :]

    def block(q_blk, blk):
        z = jnp.einsum('bhqd,bhkd->bhqk', q_blk, k).astype(jnp.float32) * scale
        q_pos = blk * Q_BLOCK + jnp.arange(Q_BLOCK)[:, None]
        strict = key_pos < q_pos
        log_fail = jnp.where(strict, jax.nn.log_sigmoid(-z), 0.0)
        after = lax.cumsum(log_fail, axis=3, reverse=True) - log_fail
        w = jnp.where(strict, jnp.exp(jax.nn.log_sigmoid(z) + after), 0.0)
        return jnp.einsum('bhqk,bhkd->bhqd', w.astype(v.dtype), v)

    return _sweep_query_blocks(block, q)


def _causal_softmax_attn(q, k, v):
    s_len = k.shape[2]
    scale = q.shape[-1] ** -0.5
    key_pos = jnp.arange(s_len)[None, :]

    def block(q_blk, blk):
        sc = jnp.einsum('bhqd,bhkd->bhqk', q_blk, k).astype(jnp.float32) * scale
        q_pos = blk * Q_BLOCK + jnp.arange(Q_BLOCK)[:, None]
        sc = jnp.where(key_pos <= q_pos, sc, -jnp.inf)
        p = jax.nn.softmax(sc, axis=-1)
        return jnp.einsum('bhqk,bhkd->bhqd', p.astype(v.dtype), v)

    return _sweep_query_blocks(block, q)


def _memory_heads(q_flat, mem, ln_g, w_kv, g_qn, g_kn):
    b, s, _ = q_flat.shape
    q = _rms(q_flat.reshape(b, s, N_MEM_HEADS, MEM_HEAD_DIM), g_qn)
    kv = (_rms(mem, ln_g) @ w_kv).reshape(b, mem.shape[1], N_MEM_HEADS, 2 * MEM_HEAD_DIM)
    k = _rms(kv[..., :MEM_HEAD_DIM], g_kn)
    v = kv[..., MEM_HEAD_DIM:]
    sc = jnp.einsum('bqhd,bkhd->bhqk', q, k).astype(jnp.float32) * (MEM_HEAD_DIM ** -0.5)
    p = jax.nn.softmax(sc, axis=-1)
    o = jnp.einsum('bhqk,bkhd->bqhd', p.astype(v.dtype), v)
    return o.reshape(b, s, MEM_Q)


def _heads_first(t, n_heads):
    b, s, _ = t.shape
    return t.reshape(b, s, n_heads, -1).transpose(0, 2, 1, 3)


def setup_inputs(seed: int = 0) -> dict:
    key = jax.random.key(seed)
    ks = iter(jax.random.split(key, 32))

    def w(shape, fan_in, extra=1.0):
        return jax.random.normal(next(ks), shape, jnp.float32) * (fan_in ** -0.5) * extra

    def gain(shape):
        return 1.0 + 0.05 * jax.random.normal(next(ks), shape, jnp.float32)

    res_scale = (2.0 * DEPTH) ** -0.5
    x = jax.random.normal(next(ks), (BATCH, SEQ, D_MODEL), jnp.float32)
    mem = jax.random.normal(next(ks), (BATCH, N_MEM, D_MODEL), jnp.float32)
    offset = jax.random.randint(next(ks), (BATCH, 1), 0, 1024, dtype=jnp.int32)
    positions = offset + jnp.arange(SEQ, dtype=jnp.int32)[None, :]
    return {
        "x": x,
        "mem": mem,
        "positions": positions,
        "ln_attn": gain((DEPTH, D_MODEL)),
        "w_out": w((DEPTH, MIX_WIDTH, D_MODEL), MIX_WIDTH, res_scale),
        "ln_mem": gain((DEPTH, D_MODEL)),
        "w_mem_kv": w((DEPTH, D_MODEL, 2 * MEM_Q), D_MODEL),
        "g_qn_mem": gain((DEPTH, MEM_HEAD_DIM)),
        "g_kn_mem": gain((DEPTH, MEM_HEAD_DIM)),
        "ln_ffn": gain((DEPTH, D_MODEL)),
        "w_ffn_gu": w((DEPTH, D_MODEL, 2 * D_FF), D_MODEL),
        "w_ffn_down": w((DEPTH, D_FF, D_MODEL), D_FF, res_scale),
        "sb_w_in": w((N_SB, D_MODEL, SB_IN), D_MODEL),
        "mla_w_in": w((N_MLA, D_MODEL, MLA_IN), D_MODEL),
        "mla_q_norm": gain((N_MLA, MLA_Q_RANK)),
        "mla_w_q_up": w((N_MLA, MLA_Q_RANK, N_MLA_HEADS * MLA_QK), MLA_Q_RANK),
        "mla_kv_norm": gain((N_MLA, MLA_KV_RANK)),
        "mla_w_kv_up": w((N_MLA, MLA_KV_RANK, N_MLA_HEADS * (MLA_NOPE + MLA_V)), MLA_KV_RANK),
        "mla_g_qn": gain((N_MLA, MLA_QK)),
        "mla_g_kn": gain((N_MLA, MLA_QK)),
    }


def reference(x, mem, positions, ln_attn, w_out, ln_mem, w_mem_kv, g_qn_mem, g_kn_mem,
              ln_ffn, w_ffn_gu, w_ffn_down, sb_w_in, mla_w_in, mla_q_norm, mla_w_q_up,
              mla_kv_norm, mla_w_kv_up, mla_g_qn, mla_g_kn):
    b, s, _ = x.shape
    for i in range(DEPTH):
        h = _rms(x, ln_attn[i])
        j = i // N_MIXERS
        if i % N_MIXERS == 0:
            proj = h @ sb_w_in[j]
            wq = N_SB_HEADS * HEAD_DIM
            q = _heads_first(proj[..., :wq], N_SB_HEADS)
            k = _heads_first(proj[..., wq:2 * wq], N_SB_HEADS)
            v = _heads_first(proj[..., 2 * wq:3 * wq], N_SB_HEADS)
            q_mem = proj[..., 3 * wq:]
            o = _stick_breaking(q, k, v)
        else:
            proj = h @ mla_w_in[j]
            c_q = proj[..., :MLA_Q_RANK]
            c_kv = proj[..., MLA_Q_RANK:MLA_Q_RANK + MLA_KV_RANK]
            k_rope = proj[..., MLA_Q_RANK + MLA_KV_RANK:MLA_Q_RANK + MLA_KV_RANK + MLA_ROPE]
            q_mem = proj[..., MLA_Q_RANK + MLA_KV_RANK + MLA_ROPE:]
            q = (_rms(c_q, mla_q_norm[j]) @ mla_w_q_up[j]).reshape(b, s, N_MLA_HEADS, MLA_QK)
            kv = (_rms(c_kv, mla_kv_norm[j]) @ mla_w_kv_up[j]).reshape(
                b, s, N_MLA_HEADS, MLA_NOPE + MLA_V)
            k_rope_h = jnp.broadcast_to(k_rope[:, :, None, :], (b, s, N_MLA_HEADS, MLA_ROPE))
            k = jnp.concatenate([kv[..., :MLA_NOPE], k_rope_h], axis=-1)
            v = kv[..., MLA_NOPE:]
            q = _rms(q, mla_g_qn[j])
            k = _rms(k, mla_g_kn[j])
            q = jnp.concatenate([q[..., :MLA_NOPE], _rope(q[..., MLA_NOPE:], positions)], -1)
            k = jnp.concatenate([k[..., :MLA_NOPE], _rope(k[..., MLA_NOPE:], positions)], -1)
            o = _causal_softmax_attn(q.transpose(0, 2, 1, 3), k.transpose(0, 2, 1, 3),
                                     v.transpose(0, 2, 1, 3))
        mixer_out = o.transpose(0, 2, 1, 3).reshape(b, s, -1)
        mem_out = _memory_heads(q_mem, mem, ln_mem[i], w_mem_kv[i], g_qn_mem[i], g_kn_mem[i])
        x = x + jnp.concatenate([mixer_out, mem_out], axis=-1) @ w_out[i]
        gu = _rms(x, ln_ffn[i]) @ w_ffn_gu[i]
        x = x + (jax.nn.silu(gu[..., :D_FF]) * gu[..., D_FF:]) @ w_ffn_down[i]
    return x
```

```python
import functools

import jax
import jax.numpy as jnp
from jax import lax
from jax.experimental import pallas as pl
from jax.experimental.pallas import tpu as pltpu

F32 = jnp.float32
BF16 = jnp.bfloat16

D_MODEL = 1024
HEAD_DIM = 64
N_HEADS = 12
N_PAIRS = N_HEADS // 2
MIX_W = N_HEADS * HEAD_DIM
MLA_Q_RANK = 768
MLA_KV_RANK = 256
MLA_NOPE = 64
MLA_ROPE = 32
MLA_QK = MLA_NOPE + MLA_ROPE
N_MEM_HEADS = 4
MEM_Q = N_MEM_HEADS * HEAD_DIM
D_FF = 2816
ROPE_THETA = 10000.0
EPS = 1e-6
LANES = 128

VMEM_LIMIT = 56 * 1024 * 1024


def _cparams(*sem):
    return pltpu.CompilerParams(dimension_semantics=sem, vmem_limit_bytes=VMEM_LIMIT)


def _rms_rows(x, g):
    ms = jnp.mean(x * x, axis=-1, keepdims=True)
    return x * lax.rsqrt(ms + EPS) * g


def _rms_matmul_kernel(x_ref, g_ref, w_ref, *o_refs, segments, chunk):
    h = _rms_rows(x_ref[...].astype(F32), g_ref[...]).astype(BF16)
    for o_ref, (start, width) in zip(o_refs, segments):
        for c in range(0, width, chunk):
            cw = min(chunk, width - c)
            acc = jnp.dot(h, w_ref[:, start + c:start + c + cw], preferred_element_type=F32)
            o_ref[:, c:c + cw] = acc.astype(o_ref.dtype)


def _rms_matmul(x, g, w, segments, dtypes, tm, name):
    t, k = x.shape
    n = w.shape[1]
    kern = functools.partial(_rms_matmul_kernel, segments=tuple(segments), chunk=512)
    return pl.pallas_call(
        kern,
        grid=(t // tm,),
        in_specs=[
            pl.BlockSpec((tm, k), lambda i: (i, 0)),
            pl.BlockSpec((1, k), lambda i: (0, 0)),
            pl.BlockSpec((k, n), lambda i: (0, 0)),
        ],
        out_specs=[pl.BlockSpec((tm, wd), lambda i: (i, 0)) for _, wd in segments],
        out_shape=[jax.ShapeDtypeStruct((t, wd), dt) for (_, wd), dt in zip(segments, dtypes)],
        compiler_params=_cparams("parallel"),
        name=name,
    )(x, g.reshape(1, k), w)


def _split3(x):
    hi = x.astype(BF16)
    r = x - hi.astype(F32)
    mid = r.astype(BF16)
    lo = (r - mid.astype(F32)).astype(BF16)
    return jnp.concatenate([hi, mid, lo], axis=-1)


def _sb_attn_kernel(q_ref, k_ref, v_ref, u_ref, o_ref, *, tq):
    qi = pl.program_id(2)
    lane = lax.broadcasted_iota(jnp.int32, (tq, LANES), 1)
    row = lax.broadcasted_iota(jnp.int32, (tq, tq), 0)
    col = lax.broadcasted_iota(jnp.int32, (tq, tq), 1)
    strict = col < row
    u3 = u_ref[...]
    q_pair = q_ref[...]
    accs = []
    for h in range(2):
        in_head = (lane >= h * HEAD_DIM) & (lane < (h + 1) * HEAD_DIM)
        q = jnp.where(in_head, q_pair * (HEAD_DIM ** -0.5), 0).astype(BF16)

        def scores(kb):
            start = pl.multiple_of(kb * tq, tq)
            k = k_ref[pl.ds(start, tq), :]
            v = v_ref[pl.ds(start, tq), :]
            z = lax.dot_general(q, k, (((1,), (1,)), ((), ())), preferred_element_type=F32)
            soft = jnp.log(1.0 + jnp.exp(-jnp.abs(z)))
            log_beta = jnp.minimum(z, 0.0) - soft
            log_fail = log_beta - z
            return log_beta, log_fail, v

        log_beta, log_fail, v = scores(qi)
        log_fail = jnp.where(strict, log_fail, 0.0)
        after = jnp.dot(_split3(log_fail), u3, preferred_element_type=F32)
        w = jnp.where(strict, jnp.exp(log_beta + after), 0.0)
        acc = jnp.dot(w.astype(BF16), v, preferred_element_type=F32)
        carry = jnp.sum(log_fail, axis=-1, keepdims=True)

        def body(i, state):
            carry, acc = state
            log_beta, log_fail, v = scores(qi - 1 - i)
            after = jnp.dot(_split3(log_fail), u3, preferred_element_type=F32) + carry
            w = jnp.exp(log_beta + after)
            acc = acc + jnp.dot(w.astype(BF16), v, preferred_element_type=F32)
            carry = carry + jnp.sum(log_fail, axis=-1, keepdims=True)
            return carry, acc

        _, acc = lax.fori_loop(0, qi, body, (carry, acc))
        accs.append(acc)
    o_ref[...] = jnp.where(lane < HEAD_DIM, accs[0], accs[1]).astype(o_ref.dtype)


def _sb_attention(qkv, b, s, tq):
    r = jnp.arange(tq)
    u = (r[:, None] > r[None, :]).astype(BF16)
    u3 = jnp.concatenate([u, u, u], axis=0)
    kern = functools.partial(_sb_attn_kernel, tq=tq)
    return pl.pallas_call(
        kern,
        grid=(b, N_PAIRS, s // tq),
        in_specs=[
            pl.BlockSpec((None, tq, LANES), lambda bi, hp, qi: (bi, qi, hp)),
            pl.BlockSpec((None, s, LANES), lambda bi, hp, qi: (bi, 0, N_PAIRS + hp)),
            pl.BlockSpec((None, s, LANES), lambda bi, hp, qi: (bi, 0, 2 * N_PAIRS + hp)),
            pl.BlockSpec((3 * tq, tq), lambda bi, hp, qi: (0, 0)),
        ],
        out_specs=pl.BlockSpec((None, tq, LANES), lambda bi, hp, qi: (bi, qi, hp)),
        out_shape=jax.ShapeDtypeStruct((b, s, MIX_W), BF16),
        compiler_params=_cparams("parallel", "parallel", "arbitrary"),
        name="sb_attention",
    )(qkv, qkv, qkv, u3)


def _rope_table_kernel(pos_ref, invf_ref, sign_ref, cos_ref, sin_ref):
    ang = pos_ref[...].astype(F32) * invf_ref[...]
    cos_ref[...] = jnp.cos(ang)
    sin_ref[...] = jnp.sin(ang) * sign_ref[...]


def _rope_tables(positions, tm):
    t = positions.size
    half = MLA_ROPE // 2
    inv_freq = ROPE_THETA ** (-jnp.arange(half, dtype=F32) / half)
    zeros = jnp.zeros((MLA_NOPE,), F32)
    tail = jnp.zeros((LANES - MLA_QK,), F32)
    invf = jnp.concatenate([zeros, inv_freq, inv_freq, tail]).reshape(1, LANES)
    ones = jnp.ones((half,), F32)
    sign = jnp.concatenate([zeros, -ones, ones, tail]).reshape(1, LANES)
    return pl.pallas_call(
        _rope_table_kernel,
        grid=(t // tm,),
        in_specs=[
            pl.BlockSpec((tm, 1), lambda i: (i, 0)),
            pl.BlockSpec((1, LANES), lambda i: (0, 0)),
            pl.BlockSpec((1, LANES), lambda i: (0, 0)),
        ],
        out_specs=[pl.BlockSpec((tm, LANES), lambda i: (i, 0))] * 2,
        out_shape=[jax.ShapeDtypeStruct((t, LANES), F32)] * 2,
        compiler_params=_cparams("parallel"),
        name="rope_tables",
    )(positions.reshape(t, 1), invf, sign)


def _head_norm_rope(x, g, cosf, sinf):
    ms = jnp.sum(x * x, axis=-1, keepdims=True) / MLA_QK
    y = x * lax.rsqrt(ms + EPS) * g
    lane = lax.broadcasted_iota(jnp.int32, y.shape, 1)
    half = MLA_ROPE // 2
    partner = jnp.where(lane < MLA_NOPE + half,
                        pltpu.roll(y, LANES - half, 1), pltpu.roll(y, half, 1))
    return y * cosf + partner * sinf


def _mla_q_kernel(cq_ref, gq_ref, w_ref, gh_ref, cos_ref, sin_ref, q_ref):
    h = _rms_rows(cq_ref[...], gq_ref[...]).astype(BF16)
    cosf = cos_ref[...]
    sinf = sin_ref[...]
    g = gh_ref[...]
    for hd in range(N_HEADS):
        sl = slice(hd * LANES, (hd + 1) * LANES)
        x = jnp.dot(h, w_ref[:, sl], preferred_element_type=F32)
        q_ref[:, sl] = _head_norm_rope(x, g, cosf, sinf).astype(q_ref.dtype)


def _mla_k_kernel(ckv_ref, gkv_ref, w_ref, kr_ref, gh_ref, cos_ref, sin_ref, k_ref, v_ref):
    h = _rms_rows(ckv_ref[...], gkv_ref[...]).astype(BF16)
    cosf = cos_ref[...]
    sinf = sin_ref[...]
    g = gh_ref[...]
    k_rope = kr_ref[...]
    for hd in range(N_HEADS):
        sl = slice(hd * LANES, (hd + 1) * LANES)
        x = jnp.dot(h, w_ref[:, sl], preferred_element_type=F32) + k_rope
        k_ref[:, sl] = _head_norm_rope(x, g, cosf, sinf).astype(k_ref.dtype)
    v_ref[...] = jnp.dot(h, w_ref[:, N_HEADS * LANES:], preferred_element_type=F32).astype(v_ref.dtype)


def _pad_head_gain(g):
    return jnp.concatenate([g.astype(F32), jnp.zeros((LANES - MLA_QK,), F32)]).reshape(1, LANES)


def _mla_prep(c_q, c_kv, k_rope, cosf, sinf, q_norm, w_q_up, kv_norm, w_kv_up, g_qn, g_kn, tm):
    t = c_q.shape[0]
    pad = LANES - MLA_QK
    wq = jnp.pad(w_q_up.reshape(MLA_Q_RANK, N_HEADS, MLA_QK), ((0, 0), (0, 0), (0, pad)))
    wq = wq.reshape(MLA_Q_RANK, N_HEADS * LANES).astype(BF16)
    wkv = w_kv_up.reshape(MLA_KV_RANK, N_HEADS, MLA_NOPE + HEAD_DIM)
    wk = jnp.pad(wkv[:, :, :MLA_NOPE], ((0, 0), (0, 0), (0, LANES - MLA_NOPE)))
    wk = wk.reshape(MLA_KV_RANK, N_HEADS * LANES)
    wv = wkv[:, :, MLA_NOPE:].reshape(MLA_KV_RANK, MIX_W)
    wkv = jnp.concatenate([wk, wv], axis=1).astype(BF16)

    row = lambda i: (i, 0)
    fixed = lambda i: (0, 0)
    q = pl.pallas_call(
        _mla_q_kernel,
        grid=(t // tm,),
        in_specs=[
            pl.BlockSpec((tm, MLA_Q_RANK), row),
            pl.BlockSpec((1, MLA_Q_RANK), fixed),
            pl.BlockSpec((MLA_Q_RANK, N_HEADS * LANES), fixed),
            pl.BlockSpec((1, LANES), fixed),
            pl.BlockSpec((tm, LANES), row),
            pl.BlockSpec((tm, LANES), row),
        ],
        out_specs=pl.BlockSpec((tm, N_HEADS * LANES), row),
        out_shape=jax.ShapeDtypeStruct((t, N_HEADS * LANES), BF16),
        compiler_params=_cparams("parallel"),
        name="mla_q_prep",
    )(c_q, q_norm.reshape(1, -1), wq, _pad_head_gain(g_qn), cosf, sinf)
    k, v = pl.pallas_call(
        _mla_k_kernel,
        grid=(t // tm,),
        in_specs=[
            pl.BlockSpec((tm, MLA_KV_RANK), row),
            pl.BlockSpec((1, MLA_KV_RANK), fixed),
            pl.BlockSpec((MLA_KV_RANK, N_HEADS * LANES + MIX_W), fixed),
            pl.BlockSpec((tm, LANES), row),
            pl.BlockSpec((1, LANES), fixed),
            pl.BlockSpec((tm, LANES), row),
            pl.BlockSpec((tm, LANES), row),
        ],
        out_specs=[pl.BlockSpec((tm, N_HEADS * LANES), row), pl.BlockSpec((tm, MIX_W), row)],
        out_shape=[jax.ShapeDtypeStruct((t, N_HEADS * LANES), BF16),
                   jax.ShapeDtypeStruct((t, MIX_W), BF16)],
        compiler_params=_cparams("parallel"),
        name="mla_kv_prep",
    )(c_kv, kv_norm.reshape(1, -1), wkv, k_rope, _pad_head_gain(g_kn), cosf, sinf)
    return q, k, v


def _causal_attn_kernel(q_ref, k_ref, v_ref, o_ref, *, tq):
    qi = pl.program_id(2)
    lane = lax.broadcasted_iota(jnp.int32, (tq, LANES), 1)
    row = lax.broadcasted_iota(jnp.int32, (tq, tq), 0)
    col = lax.broadcasted_iota(jnp.int32, (tq, tq), 1)
    causal = col <= row
    scale = MLA_QK ** -0.5
    outs = []
    for h in range(2):
        q = q_ref[:, h * LANES:(h + 1) * LANES]

        def scores(kb):
            start = pl.multiple_of(kb * tq, tq)
            k = k_ref[pl.ds(start, tq), h * LANES:(h + 1) * LANES]
            v = v_ref[pl.ds(start, tq), :]
            sc = lax.dot_general(q, k, (((1,), (1,)), ((), ())), preferred_element_type=F32)
            return sc * scale, v

        def update(state, sc, v):
            m, l, acc = state
            m_new = jnp.maximum(m, jnp.max(sc, axis=-1, keepdims=True))
            alpha = jnp.exp(m - m_new)
            p = jnp.exp(sc - m_new)
            l = alpha * l + jnp.sum(p, axis=-1, keepdims=True)
            acc = alpha * acc + jnp.dot(p.astype(BF16), v, preferred_element_type=F32)
            return m_new, l, acc

        def body(kb, state):
            sc, v = scores(kb)
            return update(state, sc, v)

        state = (jnp.full((tq, 1), -jnp.inf, F32), jnp.zeros((tq, 1), F32),
                 jnp.zeros((tq, LANES), F32))
        state = lax.fori_loop(0, qi, body, state)
        sc, v = scores(qi)
        _, l, acc = update(state, jnp.where(causal, sc, -jnp.inf), v)
        outs.append(acc / l)
    o_ref[...] = jnp.where(lane < HEAD_DIM, outs[0], outs[1]).astype(o_ref.dtype)


def _causal_attention(q, k, v, b, s, tq):
    kern = functools.partial(_causal_attn_kernel, tq=tq)
    return pl.pallas_call(
        kern,
        grid=(b, N_PAIRS, s // tq),
        in_specs=[
            pl.BlockSpec((None, tq, 2 * LANES), lambda bi, hp, qi: (bi, qi, hp)),
            pl.BlockSpec((None, s, 2 * LANES), lambda bi, hp, qi: (bi, 0, hp)),
            pl.BlockSpec((None, s, LANES), lambda bi, hp, qi: (bi, 0, hp)),
        ],
        out_specs=pl.BlockSpec((None, tq, LANES), lambda bi, hp, qi: (bi, qi, hp)),
        out_shape=jax.ShapeDtypeStruct((b, s, MIX_W), BF16),
        compiler_params=_cparams("parallel", "parallel", "arbitrary"),
        name="causal_attention",
    )(q, k, v)


def _mem_kv_kernel(kv_ref, g_ref, k_ref, v_ref):
    k = kv_ref[:, :MEM_Q]
    v = kv_ref[:, MEM_Q:]
    g = g_ref[...]
    lane = lax.broadcasted_iota(jnp.int32, k.shape, 1)
    for h in range(N_MEM_HEADS):
        in_head = (lane >= h * HEAD_DIM) & (lane < (h + 1) * HEAD_DIM)
        ms = jnp.sum(jnp.where(in_head, k * k, 0.0), axis=-1, keepdims=True) / HEAD_DIM
        k_ref[h] = jnp.where(in_head, k * lax.rsqrt(ms + EPS) * g, 0.0).astype(k_ref.dtype)
        v_ref[h] = jnp.where(in_head, v, 0.0).astype(v_ref.dtype)


def _mem_out_kernel(mix_ref, qm_ref, k_ref, v_ref, g_ref, w_ref, x_ref, o_ref):
    q = qm_ref[...]
    lane = lax.broadcasted_iota(jnp.int32, q.shape, 1)
    inv = jnp.zeros_like(q)
    for h in range(N_MEM_HEADS):
        in_head = (lane >= h * HEAD_DIM) & (lane < (h + 1) * HEAD_DIM)
        ms = jnp.sum(jnp.where(in_head, q * q, 0.0), axis=-1, keepdims=True) / HEAD_DIM
        inv = jnp.where(in_head, lax.rsqrt(ms + EPS), inv)
    qn = (q * inv * g_ref[...]).astype(BF16)
    mem_o = jnp.zeros(q.shape, F32)
    for h in range(N_MEM_HEADS):
        sc = lax.dot_general(qn, k_ref[h], (((1,), (1,)), ((), ())), preferred_element_type=F32)
        sc = sc * (HEAD_DIM ** -0.5)
        e = jnp.exp(sc - jnp.max(sc, axis=-1, keepdims=True))
        p = e / jnp.sum(e, axis=-1, keepdims=True)
        mem_o = mem_o + jnp.dot(p.astype(BF16), v_ref[h], preferred_element_type=F32)
    out = jnp.dot(mix_ref[...], w_ref[:MIX_W, :], preferred_element_type=F32)
    out = out + jnp.dot(mem_o.astype(BF16), w_ref[MIX_W:, :], preferred_element_type=F32)
    o_ref[...] = x_ref[...] + out


def _mem_kv_prep(mem2d, ln_g, w_kv, g_kn, b, n_mem):
    w = w_kv.reshape(D_MODEL, N_MEM_HEADS, 2, HEAD_DIM).transpose(0, 2, 1, 3)
    w = w.reshape(D_MODEL, 2 * MEM_Q).astype(BF16)
    (kv,) = _rms_matmul(mem2d, ln_g, w, [(0, 2 * MEM_Q)], [F32], tm=n_mem, name="mem_kv_proj")
    shape = jax.ShapeDtypeStruct((b, N_MEM_HEADS, n_mem, MEM_Q), BF16)
    spec = pl.BlockSpec((None, N_MEM_HEADS, n_mem, MEM_Q), lambda i: (i, 0, 0, 0))
    return pl.pallas_call(
        _mem_kv_kernel,
        grid=(b,),
        in_specs=[pl.BlockSpec((n_mem, 2 * MEM_Q), lambda i: (i, 0)),
                  pl.BlockSpec((1, MEM_Q), lambda i: (0, 0))],
        out_specs=[spec, spec],
        out_shape=[shape, shape],
        compiler_params=_cparams("parallel"),
        name="mem_kv_prep",
    )(kv, jnp.tile(g_kn.astype(F32), N_MEM_HEADS).reshape(1, MEM_Q))


def _mem_out_proj(mix, q_mem, k_mem, v_mem, g_qn, w_out, x, s, tm):
    t = x.shape[0]
    n_mem = k_mem.shape[2]
    per_seq = s // tm
    row = lambda i: (i, 0)
    mem_spec = pl.BlockSpec((None, N_MEM_HEADS, n_mem, MEM_Q), lambda i: (i // per_seq, 0, 0, 0))
    return pl.pallas_call(
        _mem_out_kernel,
        grid=(t // tm,),
        in_specs=[
            pl.BlockSpec((tm, MIX_W), row),
            pl.BlockSpec((tm, MEM_Q), row),
            mem_spec,
            mem_spec,
            pl.BlockSpec((1, MEM_Q), lambda i: (0, 0)),
            pl.BlockSpec((D_MODEL, D_MODEL), lambda i: (0, 0)),
            pl.BlockSpec((tm, D_MODEL), row),
        ],
        out_specs=pl.BlockSpec((tm, D_MODEL), row),
        out_shape=jax.ShapeDtypeStruct((t, D_MODEL), F32),
        compiler_params=_cparams("parallel"),
        name="mem_attn_out_proj",
    )(mix, q_mem, k_mem, v_mem, jnp.tile(g_qn.astype(F32), N_MEM_HEADS).reshape(1, MEM_Q),
      w_out, x)


def _ffn_kernel(x_ref, g_ref, wg_ref, wu_ref, wd_ref, o_ref, h_ref, acc_ref):
    j = pl.program_id(1)

    @pl.when(j == 0)
    def _():
        h_ref[...] = _rms_rows(x_ref[...], g_ref[...]).astype(BF16)
        acc_ref[...] = x_ref[...]

    h = h_ref[...]
    gate = jnp.dot(h, wg_ref[...], preferred_element_type=F32)
    up = jnp.dot(h, wu_ref[...], preferred_element_type=F32)
    act = gate * (1.0 / (1.0 + jnp.exp(-gate))) * up
    acc_ref[...] += jnp.dot(act.astype(BF16), wd_ref[...], preferred_element_type=F32)

    @pl.when(j == pl.num_programs(1) - 1)
    def _():
        o_ref[...] = acc_ref[...]


def _ffn(x, g, w_gu, w_down, tm, tf):
    t = x.shape[0]
    n_f = D_FF // tf
    return pl.pallas_call(
        _ffn_kernel,
        grid=(t // tm, n_f),
        in_specs=[
            pl.BlockSpec((tm, D_MODEL), lambda i, j: (i, 0)),
            pl.BlockSpec((1, D_MODEL), lambda i, j: (0, 0)),
            pl.BlockSpec((D_MODEL, tf), lambda i, j: (0, j)),
            pl.BlockSpec((D_MODEL, tf), lambda i, j: (0, n_f + j)),
            pl.BlockSpec((tf, D_MODEL), lambda i, j: (j, 0)),
        ],
        out_specs=pl.BlockSpec((tm, D_MODEL), lambda i, j: (i, 0)),
        out_shape=jax.ShapeDtypeStruct((t, D_MODEL), F32),
        scratch_shapes=[pltpu.VMEM((tm, D_MODEL), BF16), pltpu.VMEM((tm, D_MODEL), F32)],
        compiler_params=_cparams("parallel", "arbitrary"),
        name="swiglu_ffn",
    )(x, g.reshape(1, D_MODEL), w_gu, w_gu, w_down)


def kernel(x, mem, positions, ln_attn, w_out, ln_mem, w_mem_kv, g_qn_mem, g_kn_mem, ln_ffn,
           w_ffn_gu, w_ffn_down, sb_w_in, mla_w_in, mla_q_norm, mla_w_q_up, mla_kv_norm,
           mla_w_kv_up, mla_g_qn, mla_g_kn):
    b, s, d = x.shape
    n_mem = mem.shape[1]
    t = b * s
    x2 = x.reshape(t, d)
    mem2 = mem.reshape(b * n_mem, d)

    qkv, q_mem = _rms_matmul(x2, ln_attn[0], sb_w_in[0].astype(BF16),
                             [(0, 3 * MIX_W), (3 * MIX_W, MEM_Q)], [BF16, F32], tm=512,
                             name="sb_in_proj")
    mix = _sb_attention(qkv.reshape(b, s, 3 * MIX_W), b, s, tq=256).reshape(t, MIX_W)
    k_mem, v_mem = _mem_kv_prep(mem2, ln_mem[0], w_mem_kv[0], g_kn_mem[0], b, n_mem)
    x2 = _mem_out_proj(mix, q_mem, k_mem, v_mem, g_qn_mem[0], w_out[0].astype(BF16), x2, s, tm=512)
    x2 = _ffn(x2, ln_ffn[0], w_ffn_gu[0].astype(BF16), w_ffn_down[0].astype(BF16), tm=1024, tf=256)

    w_in = mla_w_in[0]
    o_kr = MLA_Q_RANK + MLA_KV_RANK
    o_qm = o_kr + MLA_ROPE
    w_kr = jnp.pad(w_in[:, o_kr:o_qm], ((0, 0), (MLA_NOPE, LANES - MLA_QK)))
    w_in = jnp.concatenate([w_in[:, :o_kr], w_in[:, o_qm:], w_kr], axis=1).astype(BF16)
    c_q, c_kv, q_mem, k_rope = _rms_matmul(
        x2, ln_attn[1], w_in,
        [(0, MLA_Q_RANK), (MLA_Q_RANK, MLA_KV_RANK), (o_kr, MEM_Q), (o_kr + MEM_Q, LANES)],
        [F32, F32, F32, F32], tm=512, name="mla_in_proj")
    cosf, sinf = _rope_tables(positions, tm=512)
    q, k, v = _mla_prep(c_q, c_kv, k_rope, cosf, sinf, mla_q_norm[0], mla_w_q_up[0],
                        mla_kv_norm[0], mla_w_kv_up[0], mla_g_qn[0], mla_g_kn[0], tm=512)
    mix = _causal_attention(q.reshape(b, s, -1), k.reshape(b, s, -1), v.reshape(b, s, -1),
                            b, s, tq=256).reshape(t, MIX_W)
    k_mem, v_mem = _mem_kv_prep(mem2, ln_mem[1], w_mem_kv[1], g_kn_mem[1], b, n_mem)
    x2 = _mem_out_proj(mix, q_mem, k_mem, v_mem, g_qn_mem[1], w_out[1].astype(BF16), x2, s, tm=512)
    x2 = _ffn(x2, ln_ffn[1], w_ffn_gu[1].astype(BF16), w_ffn_down[1].astype(BF16), tm=1024, tf=256)
    return x2.reshape(b, s, d)
```

```python
import functools

import jax
import jax.numpy as jnp
from jax import lax
from jax.experimental import pallas as pl
from jax.experimental.pallas import tpu as pltpu

F32 = jnp.float32
BF16 = jnp.bfloat16

D_MODEL = 1024
HEAD_DIM = 64
N_HEADS = 12
N_PAIRS = N_HEADS // 2
MIX_W = N_HEADS * HEAD_DIM
MLA_Q_RANK = 768
MLA_KV_RANK = 256
MLA_NOPE = 64
MLA_ROPE = 32
MLA_QK = MLA_NOPE + MLA_ROPE
N_MEM_HEADS = 4
MEM_Q = N_MEM_HEADS * HEAD_DIM
D_FF = 2816
ROPE_THETA = 10000.0
EPS = 1e-6
LANES = 128

VMEM_LIMIT = 56 * 1024 * 1024


def _cparams(*sem):
    return pltpu.CompilerParams(dimension_semantics=sem, vmem_limit_bytes=VMEM_LIMIT)


def _rms_rows(x, g):
    ms = jnp.mean(x * x, axis=-1, keepdims=True)
    return x * lax.rsqrt(ms + EPS) * g


def _rms_matmul_kernel(x_ref, g_ref, w_ref, *o_refs, segments, chunk):
    h = _rms_rows(x_ref[...].astype(F32), g_ref[...]).astype(BF16)
    for o_ref, (start, width) in zip(o_refs, segments):
        for c in range(0, width, chunk):
            cw = min(chunk, width - c)
            acc = jnp.dot(h, w_ref[:, start + c:start + c + cw], preferred_element_type=F32)
            o_ref[:, c:c + cw] = acc.astype(o_ref.dtype)


def _rms_matmul(x, g, w, segments, dtypes, tm, name):
    t, k = x.shape
    n = w.shape[1]
    kern = functools.partial(_rms_matmul_kernel, segments=tuple(segments), chunk=512)
    return pl.pallas_call(
        kern,
        grid=(t // tm,),
        in_specs=[
            pl.BlockSpec((tm, k), lambda i: (i, 0)),
            pl.BlockSpec((1, k), lambda i: (0, 0)),
            pl.BlockSpec((k, n), lambda i: (0, 0)),
        ],
        out_specs=[pl.BlockSpec((tm, wd), lambda i: (i, 0)) for _, wd in segments],
        out_shape=[jax.ShapeDtypeStruct((t, wd), dt) for (_, wd), dt in zip(segments, dtypes)],
        compiler_params=_cparams("parallel"),
        name=name,
    )(x, g.reshape(1, k), w)


def _split3(x):
    hi = x.astype(BF16)
    r = x - hi.astype(F32)
    mid = r.astype(BF16)
    lo = (r - mid.astype(F32)).astype(BF16)
    return jnp.concatenate([hi, mid, lo], axis=-1)


def _sb_attn_kernel(q_ref, k_ref, v_ref, u_ref, o_ref, *, tq, n_pairs):
    qi = pl.program_id(2)
    lane = lax.broadcasted_iota(jnp.int32, (tq, LANES), 1)
    row = lax.broadcasted_iota(jnp.int32, (tq, tq), 0)
    col = lax.broadcasted_iota(jnp.int32, (tq, tq), 1)
    strict = col < row
    u3 = u_ref[...]
    heads = [(p, h) for p in range(n_pairs) for h in range(2)]
    qs = []
    for p, h in heads:
        in_head = (lane >= h * HEAD_DIM) & (lane < (h + 1) * HEAD_DIM)
        q_pair = q_ref[:, p * LANES:(p + 1) * LANES]
        qs.append(jnp.where(in_head, q_pair * (HEAD_DIM ** -0.5), 0).astype(BF16))

    def block(kb, state, diagonal):
        n = len(heads)
        start = pl.multiple_of(kb * tq, tq)
        zs = []
        for idx in range(n):
            p = heads[idx][0]
            k = k_ref[pl.ds(start, tq), p * LANES:(p + 1) * LANES]
            zs.append(lax.dot_general(qs[idx], k, (((1,), (1,)), ((), ())),
                                      preferred_element_type=F32))
        log_betas, log_fails = [], []
        for z in zs:
            soft = jnp.log(1.0 + jnp.exp(-jnp.abs(z)))
            log_beta = jnp.minimum(z, 0.0) - soft
            log_fail = log_beta - z
            if diagonal:
                log_fail = jnp.where(strict, log_fail, 0.0)
            log_betas.append(log_beta)
            log_fails.append(log_fail)
        afters = [jnp.dot(_split3(lf), u3, preferred_element_type=F32) for lf in log_fails]
        ws = []
        for idx in range(n):
            w = jnp.exp(log_betas[idx] + afters[idx] + state[idx][0])
            if diagonal:
                w = jnp.where(strict, w, 0.0)
            ws.append(w.astype(BF16))
        new = []
        for idx in range(n):
            p = heads[idx][0]
            v = v_ref[pl.ds(start, tq), p * LANES:(p + 1) * LANES]
            acc = state[idx][1] + jnp.dot(ws[idx], v, preferred_element_type=F32)
            carry = state[idx][0] + jnp.sum(log_fails[idx], axis=-1, keepdims=True)
            new.append((carry, acc))
        return tuple(new)

    zero = (jnp.zeros((tq, 1), F32), jnp.zeros((tq, LANES), F32))
    state = block(qi, (zero,) * len(heads), True)
    state = lax.fori_loop(0, qi, lambda i, st: block(qi - 1 - i, st, False), state)
    for p in range(n_pairs):
        o_ref[:, p * LANES:(p + 1) * LANES] = jnp.where(
            lane < HEAD_DIM, state[2 * p][1], state[2 * p + 1][1]).astype(o_ref.dtype)


def _sb_attention(qkv, b, s, tq, n_pairs):
    r = jnp.arange(tq)
    u = (r[:, None] > r[None, :]).astype(BF16)
    u3 = jnp.concatenate([u, u, u], axis=0)
    kern = functools.partial(_sb_attn_kernel, tq=tq, n_pairs=n_pairs)
    w = n_pairs * LANES
    groups = N_PAIRS // n_pairs
    return pl.pallas_call(
        kern,
        grid=(b, groups, s // tq),
        in_specs=[
            pl.BlockSpec((None, tq, w), lambda bi, g, qi: (bi, qi, g)),
            pl.BlockSpec((None, s, w), lambda bi, g, qi: (bi, 0, groups + g)),
            pl.BlockSpec((None, s, w), lambda bi, g, qi: (bi, 0, 2 * groups + g)),
            pl.BlockSpec((3 * tq, tq), lambda bi, g, qi: (0, 0)),
        ],
        out_specs=pl.BlockSpec((None, tq, w), lambda bi, g, qi: (bi, qi, g)),
        out_shape=jax.ShapeDtypeStruct((b, s, MIX_W), BF16),
        compiler_params=_cparams("parallel", "parallel", "arbitrary"),
        name="sb_attention",
    )(qkv, qkv, qkv, u3)


def _rope_table_kernel(pos_ref, invf_ref, sign_ref, cos_ref, sin_ref):
    ang = pos_ref[...].astype(F32) * invf_ref[...]
    cos_ref[...] = jnp.cos(ang)
    sin_ref[...] = jnp.sin(ang) * sign_ref[...]


def _rope_tables(positions, tm):
    t = positions.size
    half = MLA_ROPE // 2
    inv_freq = ROPE_THETA ** (-jnp.arange(half, dtype=F32) / half)
    zeros = jnp.zeros((MLA_NOPE,), F32)
    tail = jnp.zeros((LANES - MLA_QK,), F32)
    invf = jnp.concatenate([zeros, inv_freq, inv_freq, tail]).reshape(1, LANES)
    ones = jnp.ones((half,), F32)
    sign = jnp.concatenate([zeros, -ones, ones, tail]).reshape(1, LANES)
    return pl.pallas_call(
        _rope_table_kernel,
        grid=(t // tm,),
        in_specs=[
            pl.BlockSpec((tm, 1), lambda i: (i, 0)),
            pl.BlockSpec((1, LANES), lambda i: (0, 0)),
            pl.BlockSpec((1, LANES), lambda i: (0, 0)),
        ],
        out_specs=[pl.BlockSpec((tm, LANES), lambda i: (i, 0))] * 2,
        out_shape=[jax.ShapeDtypeStruct((t, LANES), F32)] * 2,
        compiler_params=_cparams("parallel"),
        name="rope_tables",
    )(positions.reshape(t, 1), invf, sign)


def _head_norm_rope(x, g, cosf, sinf):
    ms = jnp.sum(x * x, axis=-1, keepdims=True) / MLA_QK
    y = x * lax.rsqrt(ms + EPS) * g
    lane = lax.broadcasted_iota(jnp.int32, y.shape, 1)
    half = MLA_ROPE // 2
    partner = jnp.where(lane < MLA_NOPE + half,
                        pltpu.roll(y, LANES - half, 1), pltpu.roll(y, half, 1))
    return y * cosf + partner * sinf


def _mla_q_kernel(cq_ref, gq_ref, w_ref, gh_ref, cos_ref, sin_ref, q_ref):
    h = _rms_rows(cq_ref[...], gq_ref[...]).astype(BF16)
    cosf = cos_ref[...]
    sinf = sin_ref[...]
    g = gh_ref[...]
    for hd in range(N_HEADS):
        sl = slice(hd * LANES, (hd + 1) * LANES)
        x = jnp.dot(h, w_ref[:, sl], preferred_element_type=F32)
        q_ref[:, sl] = _head_norm_rope(x, g, cosf, sinf).astype(q_ref.dtype)


def _mla_k_kernel(ckv_ref, gkv_ref, w_ref, kr_ref, gh_ref, cos_ref, sin_ref, k_ref, v_ref):
    h = _rms_rows(ckv_ref[...], gkv_ref[...]).astype(BF16)
    cosf = cos_ref[...]
    sinf = sin_ref[...]
    g = gh_ref[...]
    k_rope = kr_ref[...]
    for hd in range(N_HEADS):
        sl = slice(hd * LANES, (hd + 1) * LANES)
        x = jnp.dot(h, w_ref[:, sl], preferred_element_type=F32) + k_rope
        k_ref[:, sl] = _head_norm_rope(x, g, cosf, sinf).astype(k_ref.dtype)
    v_ref[...] = jnp.dot(h, w_ref[:, N_HEADS * LANES:], preferred_element_type=F32).astype(v_ref.dtype)


def _pad_head_gain(g):
    return jnp.concatenate([g.astype(F32), jnp.zeros((LANES - MLA_QK,), F32)]).reshape(1, LANES)


def _mla_prep(c_q, c_kv, k_rope, cosf, sinf, q_norm, w_q_up, kv_norm, w_kv_up, g_qn, g_kn, tm):
    t = c_q.shape[0]
    pad = LANES - MLA_QK
    wq = jnp.pad(w_q_up.reshape(MLA_Q_RANK, N_HEADS, MLA_QK), ((0, 0), (0, 0), (0, pad)))
    wq = wq.reshape(MLA_Q_RANK, N_HEADS * LANES).astype(BF16)
    wkv = w_kv_up.reshape(MLA_KV_RANK, N_HEADS, MLA_NOPE + HEAD_DIM)
    wk = jnp.pad(wkv[:, :, :MLA_NOPE], ((0, 0), (0, 0), (0, LANES - MLA_NOPE)))
    wk = wk.reshape(MLA_KV_RANK, N_HEADS * LANES)
    wv = wkv[:, :, MLA_NOPE:].reshape(MLA_KV_RANK, MIX_W)
    wkv = jnp.concatenate([wk, wv], axis=1).astype(BF16)

    row = lambda i: (i, 0)
    fixed = lambda i: (0, 0)
    q = pl.pallas_call(
        _mla_q_kernel,
        grid=(t // tm,),
        in_specs=[
            pl.BlockSpec((tm, MLA_Q_RANK), row),
            pl.BlockSpec((1, MLA_Q_RANK), fixed),
            pl.BlockSpec((MLA_Q_RANK, N_HEADS * LANES), fixed),
            pl.BlockSpec((1, LANES), fixed),
            pl.BlockSpec((tm, LANES), row),
            pl.BlockSpec((tm, LANES), row),
        ],
        out_specs=pl.BlockSpec((tm, N_HEADS * LANES), row),
        out_shape=jax.ShapeDtypeStruct((t, N_HEADS * LANES), BF16),
        compiler_params=_cparams("parallel"),
        name="mla_q_prep",
    )(c_q, q_norm.reshape(1, -1), wq, _pad_head_gain(g_qn), cosf, sinf)
    k, v = pl.pallas_call(
        _mla_k_kernel,
        grid=(t // tm,),
        in_specs=[
            pl.BlockSpec((tm, MLA_KV_RANK), row),
            pl.BlockSpec((1, MLA_KV_RANK), fixed),
            pl.BlockSpec((MLA_KV_RANK, N_HEADS * LANES + MIX_W), fixed),
            pl.BlockSpec((tm, LANES), row),
            pl.BlockSpec((1, LANES), fixed),
            pl.BlockSpec((tm, LANES), row),
            pl.BlockSpec((tm, LANES), row),
        ],
        out_specs=[pl.BlockSpec((tm, N_HEADS * LANES), row), pl.BlockSpec((tm, MIX_W), row)],
        out_shape=[jax.ShapeDtypeStruct((t, N_HEADS * LANES), BF16),
                   jax.ShapeDtypeStruct((t, MIX_W), BF16)],
        compiler_params=_cparams("parallel"),
        name="mla_kv_prep",
    )(c_kv, kv_norm.reshape(1, -1), wkv, k_rope, _pad_head_gain(g_kn), cosf, sinf)
    return q, k, v


def _causal_attn_kernel(q_ref, k_ref, v_ref, o_ref, *, tq, n_pairs):
    qi = pl.program_id(2)
    lane = lax.broadcasted_iota(jnp.int32, (tq, LANES), 1)
    row = lax.broadcasted_iota(jnp.int32, (tq, tq), 0)
    col = lax.broadcasted_iota(jnp.int32, (tq, tq), 1)
    causal = col <= row
    scale = MLA_QK ** -0.5
    n_heads = 2 * n_pairs
    qs = [q_ref[:, h * LANES:(h + 1) * LANES] for h in range(n_heads)]

    def block(kb, state, diagonal):
        start = pl.multiple_of(kb * tq, tq)
        scs = []
        for h in range(n_heads):
            k = k_ref[pl.ds(start, tq), h * LANES:(h + 1) * LANES]
            scs.append(lax.dot_general(qs[h], k, (((1,), (1,)), ((), ())),
                                       preferred_element_type=F32))
        ms, ls, alphas, ps = [], [], [], []
        for h in range(n_heads):
            m, l, _ = state[h]
            sc = scs[h] * scale
            if diagonal:
                sc = jnp.where(causal, sc, -jnp.inf)
            m_new = jnp.maximum(m, jnp.max(sc, axis=-1, keepdims=True))
            alpha = jnp.exp(m - m_new)
            p = jnp.exp(sc - m_new)
            ms.append(m_new)
            ls.append(alpha * l + jnp.sum(p, axis=-1, keepdims=True))
            alphas.append(alpha)
            ps.append(p.astype(BF16))
        new = []
        for h in range(n_heads):
            v = v_ref[pl.ds(start, tq), (h // 2) * LANES:(h // 2 + 1) * LANES]
            acc = alphas[h] * state[h][2] + jnp.dot(ps[h], v, preferred_element_type=F32)
            new.append((ms[h], ls[h], acc))
        return tuple(new)

    init = (jnp.full((tq, 1), -jnp.inf, F32), jnp.zeros((tq, 1), F32), jnp.zeros((tq, LANES), F32))
    state = lax.fori_loop(0, qi, lambda kb, st: block(kb, st, False), (init,) * n_heads)
    state = block(qi, state, True)
    outs = [acc / l for _, l, acc in state]
    for p in range(n_pairs):
        o_ref[:, p * LANES:(p + 1) * LANES] = jnp.where(
            lane < HEAD_DIM, outs[2 * p], outs[2 * p + 1]).astype(o_ref.dtype)


def _causal_attention(q, k, v, b, s, tq, n_pairs):
    kern = functools.partial(_causal_attn_kernel, tq=tq, n_pairs=n_pairs)
    wq = 2 * n_pairs * LANES
    wv = n_pairs * LANES
    return pl.pallas_call(
        kern,
        grid=(b, N_PAIRS // n_pairs, s // tq),
        in_specs=[
            pl.BlockSpec((None, tq, wq), lambda bi, g, qi: (bi, qi, g)),
            pl.BlockSpec((None, s, wq), lambda bi, g, qi: (bi, 0, g)),
            pl.BlockSpec((None, s, wv), lambda bi, g, qi: (bi, 0, g)),
        ],
        out_specs=pl.BlockSpec((None, tq, wv), lambda bi, g, qi: (bi, qi, g)),
        out_shape=jax.ShapeDtypeStruct((b, s, MIX_W), BF16),
        compiler_params=_cparams("parallel", "parallel", "arbitrary"),
        name="causal_attention",
    )(q, k, v)


def _mem_kv_kernel(kv_ref, g_ref, k_ref, v_ref):
    k = kv_ref[:, :MEM_Q]
    v = kv_ref[:, MEM_Q:]
    g = g_ref[...]
    lane = lax.broadcasted_iota(jnp.int32, k.shape, 1)
    for h in range(N_MEM_HEADS):
        in_head = (lane >= h * HEAD_DIM) & (lane < (h + 1) * HEAD_DIM)
        ms = jnp.sum(jnp.where(in_head, k * k, 0.0), axis=-1, keepdims=True) / HEAD_DIM
        k_ref[h] = jnp.where(in_head, k * lax.rsqrt(ms + EPS) * g, 0.0).astype(k_ref.dtype)
        v_ref[h] = jnp.where(in_head, v, 0.0).astype(v_ref.dtype)


def _mem_out_kernel(mix_ref, qm_ref, k_ref, v_ref, g_ref, w_ref, x_ref, o_ref):
    q = qm_ref[...]
    lane = lax.broadcasted_iota(jnp.int32, q.shape, 1)
    inv = jnp.zeros_like(q)
    for h in range(N_MEM_HEADS):
        in_head = (lane >= h * HEAD_DIM) & (lane < (h + 1) * HEAD_DIM)
        ms = jnp.sum(jnp.where(in_head, q * q, 0.0), axis=-1, keepdims=True) / HEAD_DIM
        inv = jnp.where(in_head, lax.rsqrt(ms + EPS), inv)
    qn = (q * inv * g_ref[...]).astype(BF16)
    mem_o = jnp.zeros(q.shape, F32)
    for h in range(N_MEM_HEADS):
        sc = lax.dot_general(qn, k_ref[h], (((1,), (1,)), ((), ())), preferred_element_type=F32)
        sc = sc * (HEAD_DIM ** -0.5)
        e = jnp.exp(sc - jnp.max(sc, axis=-1, keepdims=True))
        p = e / jnp.sum(e, axis=-1, keepdims=True)
        mem_o = mem_o + jnp.dot(p.astype(BF16), v_ref[h], preferred_element_type=F32)
    out = jnp.dot(mix_ref[...], w_ref[:MIX_W, :], preferred_element_type=F32)
    out = out + jnp.dot(mem_o.astype(BF16), w_ref[MIX_W:, :], preferred_element_type=F32)
    o_ref[...] = x_ref[...] + out


def _mem_kv_prep(mem2d, ln_g, w_kv, g_kn, b, n_mem):
    w = w_kv.reshape(D_MODEL, N_MEM_HEADS, 2, HEAD_DIM).transpose(0, 2, 1, 3)
    w = w.reshape(D_MODEL, 2 * MEM_Q).astype(BF16)
    (kv,) = _rms_matmul(mem2d, ln_g, w, [(0, 2 * MEM_Q)], [F32], tm=n_mem, name="mem_kv_proj")
    shape = jax.ShapeDtypeStruct((b, N_MEM_HEADS, n_mem, MEM_Q), BF16)
    spec = pl.BlockSpec((None, N_MEM_HEADS, n_mem, MEM_Q), lambda i: (i, 0, 0, 0))
    return pl.pallas_call(
        _mem_kv_kernel,
        grid=(b,),
        in_specs=[pl.BlockSpec((n_mem, 2 * MEM_Q), lambda i: (i, 0)),
                  pl.BlockSpec((1, MEM_Q), lambda i: (0, 0))],
        out_specs=[spec, spec],
        out_shape=[shape, shape],
        compiler_params=_cparams("parallel"),
        name="mem_kv_prep",
    )(kv, jnp.tile(g_kn.astype(F32), N_MEM_HEADS).reshape(1, MEM_Q))


def _mem_out_proj(mix, q_mem, k_mem, v_mem, g_qn, w_out, x, s, tm):
    t = x.shape[0]
    n_mem = k_mem.shape[2]
    per_seq = s // tm
    row = lambda i: (i, 0)
    mem_spec = pl.BlockSpec((None, N_MEM_HEADS, n_mem, MEM_Q), lambda i: (i // per_seq, 0, 0, 0))
    return pl.pallas_call(
        _mem_out_kernel,
        grid=(t // tm,),
        in_specs=[
            pl.BlockSpec((tm, MIX_W), row),
            pl.BlockSpec((tm, MEM_Q), row),
            mem_spec,
            mem_spec,
            pl.BlockSpec((1, MEM_Q), lambda i: (0, 0)),
            pl.BlockSpec((D_MODEL, D_MODEL), lambda i: (0, 0)),
            pl.BlockSpec((tm, D_MODEL), row),
        ],
        out_specs=pl.BlockSpec((tm, D_MODEL), row),
        out_shape=jax.ShapeDtypeStruct((t, D_MODEL), F32),
        compiler_params=_cparams("parallel"),
        name="mem_attn_out_proj",
    )(mix, q_mem, k_mem, v_mem, jnp.tile(g_qn.astype(F32), N_MEM_HEADS).reshape(1, MEM_Q),
      w_out, x)


def _ffn_kernel(x_ref, g_ref, wg_ref, wu_ref, wd_ref, o_ref, h_ref, acc_ref):
    j = pl.program_id(1)

    @pl.when(j == 0)
    def _():
        h_ref[...] = _rms_rows(x_ref[...], g_ref[...]).astype(BF16)
        acc_ref[...] = x_ref[...]

    h = h_ref[...]
    gate = jnp.dot(h, wg_ref[...], preferred_element_type=F32)
    up = jnp.dot(h, wu_ref[...], preferred_element_type=F32)
    act = gate * (1.0 / (1.0 + jnp.exp(-gate))) * up
    acc_ref[...] += jnp.dot(act.astype(BF16), wd_ref[...], preferred_element_type=F32)

    @pl.when(j == pl.num_programs(1) - 1)
    def _():
        o_ref[...] = acc_ref[...]


def _ffn(x, g, w_gu, w_down, tm, tf):
    t = x.shape[0]
    n_f = D_FF // tf
    return pl.pallas_call(
        _ffn_kernel,
        grid=(t // tm, n_f),
        in_specs=[
            pl.BlockSpec((tm, D_MODEL), lambda i, j: (i, 0)),
            pl.BlockSpec((1, D_MODEL), lambda i, j: (0, 0)),
            pl.BlockSpec((D_MODEL, tf), lambda i, j: (0, j)),
            pl.BlockSpec((D_MODEL, tf), lambda i, j: (0, n_f + j)),
            pl.BlockSpec((tf, D_MODEL), lambda i, j: (j, 0)),
        ],
        out_specs=pl.BlockSpec((tm, D_MODEL), lambda i, j: (i, 0)),
        out_shape=jax.ShapeDtypeStruct((t, D_MODEL), F32),
        scratch_shapes=[pltpu.VMEM((tm, D_MODEL), BF16), pltpu.VMEM((tm, D_MODEL), F32)],
        compiler_params=_cparams("parallel", "arbitrary"),
        name="swiglu_ffn",
    )(x, g.reshape(1, D_MODEL), w_gu, w_gu, w_down)


def kernel(x, mem, positions, ln_attn, w_out, ln_mem, w_mem_kv, g_qn_mem, g_kn_mem, ln_ffn,
           w_ffn_gu, w_ffn_down, sb_w_in, mla_w_in, mla_q_norm, mla_w_q_up, mla_kv_norm,
           mla_w_kv_up, mla_g_qn, mla_g_kn):
    b, s, d = x.shape
    n_mem = mem.shape[1]
    t = b * s
    x2 = x.reshape(t, d)
    mem2 = mem.reshape(b * n_mem, d)

    qkv, q_mem = _rms_matmul(x2, ln_attn[0], sb_w_in[0].astype(BF16),
                             [(0, 3 * MIX_W), (3 * MIX_W, MEM_Q)], [BF16, F32], tm=512,
                             name="sb_in_proj")
    mix = _sb_attention(qkv.reshape(b, s, 3 * MIX_W), b, s, tq=256, n_pairs=2).reshape(t, MIX_W)
    k_mem, v_mem = _mem_kv_prep(mem2, ln_mem[0], w_mem_kv[0], g_kn_mem[0], b, n_mem)
    x2 = _mem_out_proj(mix, q_mem, k_mem, v_mem, g_qn_mem[0], w_out[0].astype(BF16), x2, s, tm=512)
    x2 = _ffn(x2, ln_ffn[0], w_ffn_gu[0].astype(BF16), w_ffn_down[0].astype(BF16), tm=1024, tf=256)

    w_in = mla_w_in[0]
    o_kr = MLA_Q_RANK + MLA_KV_RANK
    o_qm = o_kr + MLA_ROPE
    w_kr = jnp.pad(w_in[:, o_kr:o_qm], ((0, 0), (MLA_NOPE, LANES - MLA_QK)))
    w_in = jnp.concatenate([w_in[:, :o_kr], w_in[:, o_qm:], w_kr], axis=1).astype(BF16)
    c_q, c_kv, q_mem, k_rope = _rms_matmul(
        x2, ln_attn[1], w_in,
        [(0, MLA_Q_RANK), (MLA_Q_RANK, MLA_KV_RANK), (o_kr, MEM_Q), (o_kr + MEM_Q, LANES)],
        [F32, F32, F32, F32], tm=512, name="mla_in_proj")
    cosf, sinf = _rope_tables(positions, tm=512)
    q, k, v = _mla_prep(c_q, c_kv, k_rope, cosf, sinf, mla_q_norm[0], mla_w_q_up[0],
                        mla_kv_norm[0], mla_w_kv_up[0], mla_g_qn[0], mla_g_kn[0], tm=512)
    mix = _causal_attention(q.reshape(b, s, -1), k.reshape(b, s, -1), v.reshape(b, s, -1),
                            b, s, tq=256, n_pairs=2).reshape(t, MIX_W)
    k_mem, v_mem = _mem_kv_prep(mem2, ln_mem[1], w_mem_kv[1], g_kn_mem[1], b, n_mem)
    x2 = _mem_out_proj(mix, q_mem, k_mem, v_mem, g_qn_mem[1], w_out[1].astype(BF16), x2, s, tm=512)
    x2 = _ffn(x2, ln_ffn[1], w_ffn_gu[1].astype(BF16), w_ffn_down[1].astype(BF16), tm=1024, tf=256)
    return x2.reshape(b, s, d)
```

```python
import functools

import jax
import jax.numpy as jnp
from jax import lax
from jax.experimental import pallas as pl
from jax.experimental.pallas import tpu as pltpu

F32 = jnp.float32
BF16 = jnp.bfloat16

D_MODEL = 1024
HEAD_DIM = 64
N_HEADS = 12
N_PAIRS = N_HEADS // 2
MIX_W = N_HEADS * HEAD_DIM
MLA_Q_RANK = 768
MLA_KV_RANK = 256
MLA_NOPE = 64
MLA_ROPE = 32
MLA_QK = MLA_NOPE + MLA_ROPE
N_MEM_HEADS = 4
MEM_Q = N_MEM_HEADS * HEAD_DIM
D_FF = 2816
ROPE_THETA = 10000.0
EPS = 1e-6
LOG2_E = 1.4426950408889634
LANES = 128

VMEM_LIMIT = 56 * 1024 * 1024


def _cparams(*sem):
    return pltpu.CompilerParams(dimension_semantics=sem, vmem_limit_bytes=VMEM_LIMIT)


def _rms_rows(x, g):
    ms = jnp.mean(x * x, axis=-1, keepdims=True)
    return x * lax.rsqrt(ms + EPS) * g


def _rms_matmul_kernel(x_ref, g_ref, w_ref, *o_refs, segments, chunk):
    h = _rms_rows(x_ref[...].astype(F32), g_ref[...]).astype(BF16)
    for o_ref, (start, width) in zip(o_refs, segments):
        for c in range(0, width, chunk):
            cw = min(chunk, width - c)
            acc = jnp.dot(h, w_ref[:, start + c:start + c + cw], preferred_element_type=F32)
            o_ref[:, c:c + cw] = acc.astype(o_ref.dtype)


def _rms_matmul(x, g, w, segments, dtypes, tm, name):
    t, k = x.shape
    n = w.shape[1]
    kern = functools.partial(_rms_matmul_kernel, segments=tuple(segments), chunk=512)
    return pl.pallas_call(
        kern,
        grid=(t // tm,),
        in_specs=[
            pl.BlockSpec((tm, k), lambda i: (i, 0)),
            pl.BlockSpec((1, k), lambda i: (0, 0)),
            pl.BlockSpec((k, n), lambda i: (0, 0)),
        ],
        out_specs=[pl.BlockSpec((tm, wd), lambda i: (i, 0)) for _, wd in segments],
        out_shape=[jax.ShapeDtypeStruct((t, wd), dt) for (_, wd), dt in zip(segments, dtypes)],
        compiler_params=_cparams("parallel"),
        name=name,
    )(x, g.reshape(1, k), w)


N_SPLIT = 2
U_EXTRA = 16


def _sb_in_proj_kernel(x_ref, g_ref, w_ref, wvt_ref, qk_ref, qm_ref, vt_ref):
    h = _rms_rows(x_ref[...], g_ref[...]).astype(BF16)
    n_qk = qk_ref.shape[1]
    for c in range(0, n_qk, 512):
        qk_ref[:, c:c + 512] = jnp.dot(h, w_ref[:, c:c + 512],
                                       preferred_element_type=F32).astype(qk_ref.dtype)
    qm_ref[...] = jnp.dot(h, w_ref[:, n_qk:], preferred_element_type=F32)
    vt = lax.dot_general(wvt_ref[...], h, (((1,), (1,)), ((), ())), preferred_element_type=F32)
    tk = vt_ref.shape[-1]
    for c in range(vt_ref.shape[0]):
        vt_ref[c] = vt[:, c * tk:(c + 1) * tk].astype(vt_ref.dtype)


def _sb_in_proj(x, g, w_in, b, s, tm, tk):
    t = x.shape[0]
    w_qk_qm = jnp.concatenate([w_in[:, :2 * MIX_W], w_in[:, 3 * MIX_W:]], axis=1).astype(BF16)
    wvt = w_in[:, 2 * MIX_W:3 * MIX_W].T.astype(BF16)
    per_seq = s // tm
    row = lambda i: (i, 0)
    fixed = lambda i: (0, 0)
    return pl.pallas_call(
        _sb_in_proj_kernel,
        grid=(t // tm,),
        in_specs=[
            pl.BlockSpec((tm, D_MODEL), row),
            pl.BlockSpec((1, D_MODEL), fixed),
            pl.BlockSpec((D_MODEL, 2 * MIX_W + MEM_Q), fixed),
            pl.BlockSpec((MIX_W, D_MODEL), fixed),
        ],
        out_specs=[pl.BlockSpec((tm, 2 * MIX_W), row), pl.BlockSpec((tm, MEM_Q), row),
                   pl.BlockSpec((None, tm // tk, MIX_W, tk),
                                lambda i: (i // per_seq, i % per_seq, 0, 0))],
        out_shape=[jax.ShapeDtypeStruct((t, 2 * MIX_W), BF16),
                   jax.ShapeDtypeStruct((t, MEM_Q), F32),
                   jax.ShapeDtypeStruct((b, s // tk, MIX_W, tk), BF16)],
        compiler_params=_cparams("parallel"),
        name="sb_in_proj",
    )(x, g.reshape(1, D_MODEL), w_qk_qm, wvt)


def _split_rows(x):
    parts = []
    r = x
    for i in range(N_SPLIT):
        piece = r.astype(BF16)
        parts.append(piece)
        if i + 1 < N_SPLIT:
            r = r - piece.astype(F32)
    return jnp.concatenate(parts, axis=0)


def _sb_attn_kernel(q_ref, k_ref, vt_ref, u_ref, o_ref, *, tq, n_pairs):
    qi = pl.program_id(2)
    lane = lax.broadcasted_iota(jnp.int32, (tq, LANES), 1)
    key = lax.broadcasted_iota(jnp.int32, (tq, tq), 0)
    qry = lax.broadcasted_iota(jnp.int32, (tq, tq), 1)
    strict = key < qry
    u = u_ref[...]
    heads = [(p, h) for p in range(n_pairs) for h in range(2)]
    n = len(heads)
    qs = []
    for p, h in heads:
        in_head = (lane >= h * HEAD_DIM) & (lane < (h + 1) * HEAD_DIM)
        q_pair = q_ref[:, p * LANES:(p + 1) * LANES]
        qs.append(jnp.where(in_head, q_pair * (HEAD_DIM ** -0.5), 0).astype(BF16))

    def block(kb, state, diagonal):
        start = pl.multiple_of(kb * tq, tq)
        zs = []
        for idx in range(n):
            p = heads[idx][0]
            k = k_ref[pl.ds(start, tq), p * LANES:(p + 1) * LANES]
            zs.append(lax.dot_general(k, qs[idx], (((1,), (1,)), ((), ())),
                                      preferred_element_type=F32))
        log_betas, log_fails = [], []
        for z in zs:
            soft = jnp.log(1.0 + jnp.exp(-jnp.abs(z)))
            log_beta = jnp.minimum(z, 0.0) - soft
            log_fail = log_beta - z
            if diagonal:
                log_fail = jnp.where(strict, log_fail, 0.0)
            log_betas.append(log_beta)
            log_fails.append(log_fail)
        sums = [jnp.dot(u, _split_rows(lf), preferred_element_type=F32) for lf in log_fails]
        ws = []
        for idx in range(n):
            w = jnp.exp(log_betas[idx] + sums[idx][:tq] + state[idx][0])
            if diagonal:
                w = jnp.where(strict, w, 0.0)
            ws.append(w.astype(BF16))
        new = []
        for idx in range(n):
            vt = vt_ref[kb, idx * HEAD_DIM:(idx + 1) * HEAD_DIM, :]
            acc = state[idx][1] + jnp.dot(vt, ws[idx], preferred_element_type=F32)
            carry = state[idx][0] + sums[idx][tq:tq + 1]
            new.append((carry, acc))
        return tuple(new)

    zero = (jnp.zeros((1, tq), F32), jnp.zeros((HEAD_DIM, tq), F32))
    state = block(qi, (zero,) * n, True)
    state = lax.fori_loop(0, qi, lambda i, st: block(qi - 1 - i, st, False), state)
    for p in range(n_pairs):
        pair = jnp.concatenate([state[2 * p][1], state[2 * p + 1][1]], axis=0)
        o_ref[:, p * LANES:(p + 1) * LANES] = pair.T.astype(o_ref.dtype)


def _sb_attention(qk, vt, b, s, tq, n_pairs):
    r = jnp.arange(tq)
    u = (r[None, :] > r[:, None]).astype(BF16)
    u = jnp.concatenate([u, jnp.ones((U_EXTRA, tq), BF16)], axis=0)
    u = jnp.concatenate([u] * N_SPLIT, axis=1)
    kern = functools.partial(_sb_attn_kernel, tq=tq, n_pairs=n_pairs)
    w = n_pairs * LANES
    groups = N_PAIRS // n_pairs
    return pl.pallas_call(
        kern,
        grid=(b, groups, s // tq),
        in_specs=[
            pl.BlockSpec((None, tq, w), lambda bi, g, qi: (bi, qi, g)),
            pl.BlockSpec((None, s, w), lambda bi, g, qi: (bi, 0, groups + g)),
            pl.BlockSpec((None, s // tq, w, tq), lambda bi, g, qi: (bi, 0, g, 0)),
            pl.BlockSpec((tq + U_EXTRA, N_SPLIT * tq), lambda bi, g, qi: (0, 0)),
        ],
        out_specs=pl.BlockSpec((None, tq, w), lambda bi, g, qi: (bi, qi, g)),
        out_shape=jax.ShapeDtypeStruct((b, s, MIX_W), BF16),
        compiler_params=_cparams("parallel", "parallel", "arbitrary"),
        name="sb_attention",
    )(qk, qk, vt, u)


def _rope_table_kernel(pos_ref, invf_ref, sign_ref, cos_ref, sin_ref):
    ang = pos_ref[...].astype(F32) * invf_ref[...]
    cos_ref[...] = jnp.cos(ang)
    sin_ref[...] = jnp.sin(ang) * sign_ref[...]


def _rope_tables(positions, tm):
    t = positions.size
    half = MLA_ROPE // 2
    inv_freq = ROPE_THETA ** (-jnp.arange(half, dtype=F32) / half)
    zeros = jnp.zeros((MLA_NOPE,), F32)
    tail = jnp.zeros((LANES - MLA_QK,), F32)
    invf = jnp.concatenate([zeros, inv_freq, inv_freq, tail]).reshape(1, LANES)
    ones = jnp.ones((half,), F32)
    sign = jnp.concatenate([zeros, -ones, ones, tail]).reshape(1, LANES)
    return pl.pallas_call(
        _rope_table_kernel,
        grid=(t // tm,),
        in_specs=[
            pl.BlockSpec((tm, 1), lambda i: (i, 0)),
            pl.BlockSpec((1, LANES), lambda i: (0, 0)),
            pl.BlockSpec((1, LANES), lambda i: (0, 0)),
        ],
        out_specs=[pl.BlockSpec((tm, LANES), lambda i: (i, 0))] * 2,
        out_shape=[jax.ShapeDtypeStruct((t, LANES), F32)] * 2,
        compiler_params=_cparams("parallel"),
        name="rope_tables",
    )(positions.reshape(t, 1), invf, sign)


def _head_norm_rope(x, g, cosf, sinf):
    ms = jnp.sum(x * x, axis=-1, keepdims=True) / MLA_QK
    y = x * lax.rsqrt(ms + EPS) * g
    lane = lax.broadcasted_iota(jnp.int32, y.shape, 1)
    half = MLA_ROPE // 2
    partner = jnp.where(lane < MLA_NOPE + half,
                        pltpu.roll(y, LANES - half, 1), pltpu.roll(y, half, 1))
    return y * cosf + partner * sinf


def _mla_q_kernel(cq_ref, gq_ref, w_ref, gh_ref, cos_ref, sin_ref, q_ref):
    h = _rms_rows(cq_ref[...], gq_ref[...]).astype(BF16)
    cosf = cos_ref[...]
    sinf = sin_ref[...]
    g = gh_ref[...]
    for hd in range(N_HEADS):
        sl = slice(hd * LANES, (hd + 1) * LANES)
        x = jnp.dot(h, w_ref[:, sl], preferred_element_type=F32)
        q_ref[:, sl] = _head_norm_rope(x, g, cosf, sinf).astype(q_ref.dtype)


def _mla_k_kernel(ckv_ref, gkv_ref, w_ref, wvt_ref, kr_ref, gh_ref, cos_ref, sin_ref,
                  k_ref, vt_ref):
    h = _rms_rows(ckv_ref[...], gkv_ref[...]).astype(BF16)
    cosf = cos_ref[...]
    sinf = sin_ref[...]
    g = gh_ref[...]
    k_rope = kr_ref[...]
    for hd in range(N_HEADS):
        sl = slice(hd * LANES, (hd + 1) * LANES)
        x = jnp.dot(h, w_ref[:, sl], preferred_element_type=F32) + k_rope
        k_ref[:, sl] = _head_norm_rope(x, g, cosf, sinf).astype(k_ref.dtype)
    vt = lax.dot_general(wvt_ref[...], h, (((1,), (1,)), ((), ())), preferred_element_type=F32)
    tk = vt_ref.shape[-1]
    for c in range(vt_ref.shape[0]):
        vt_ref[c] = vt[:, c * tk:(c + 1) * tk].astype(vt_ref.dtype)


def _pad_head_gain(g):
    return jnp.concatenate([g.astype(F32), jnp.zeros((LANES - MLA_QK,), F32)]).reshape(1, LANES)


def _mla_prep(c_q, c_kv, k_rope, cosf, sinf, q_norm, w_q_up, kv_norm, w_kv_up, g_qn, g_kn,
              b, s, tm, tk):
    t = c_q.shape[0]
    pad = LANES - MLA_QK
    wq = jnp.pad(w_q_up.reshape(MLA_Q_RANK, N_HEADS, MLA_QK), ((0, 0), (0, 0), (0, pad)))
    wq = wq.reshape(MLA_Q_RANK, N_HEADS * LANES).astype(BF16)
    wkv = w_kv_up.reshape(MLA_KV_RANK, N_HEADS, MLA_NOPE + HEAD_DIM)
    wk = jnp.pad(wkv[:, :, :MLA_NOPE], ((0, 0), (0, 0), (0, LANES - MLA_NOPE)))
    wk = wk.reshape(MLA_KV_RANK, N_HEADS * LANES).astype(BF16)
    wvt = wkv[:, :, MLA_NOPE:].reshape(MLA_KV_RANK, MIX_W).T.astype(BF16)

    row = lambda i: (i, 0)
    fixed = lambda i: (0, 0)
    q = pl.pallas_call(
        _mla_q_kernel,
        grid=(t // tm,),
        in_specs=[
            pl.BlockSpec((tm, MLA_Q_RANK), row),
            pl.BlockSpec((1, MLA_Q_RANK), fixed),
            pl.BlockSpec((MLA_Q_RANK, N_HEADS * LANES), fixed),
            pl.BlockSpec((1, LANES), fixed),
            pl.BlockSpec((tm, LANES), row),
            pl.BlockSpec((tm, LANES), row),
        ],
        out_specs=pl.BlockSpec((tm, N_HEADS * LANES), row),
        out_shape=jax.ShapeDtypeStruct((t, N_HEADS * LANES), BF16),
        compiler_params=_cparams("parallel"),
        name="mla_q_prep",
    )(c_q, q_norm.reshape(1, -1), wq, _pad_head_gain(g_qn), cosf, sinf)
    per_seq = s // tm
    k, vt = pl.pallas_call(
        _mla_k_kernel,
        grid=(t // tm,),
        in_specs=[
            pl.BlockSpec((tm, MLA_KV_RANK), row),
            pl.BlockSpec((1, MLA_KV_RANK), fixed),
            pl.BlockSpec((MLA_KV_RANK, N_HEADS * LANES), fixed),
            pl.BlockSpec((MIX_W, MLA_KV_RANK), fixed),
            pl.BlockSpec((tm, LANES), row),
            pl.BlockSpec((1, LANES), fixed),
            pl.BlockSpec((tm, LANES), row),
            pl.BlockSpec((tm, LANES), row),
        ],
        out_specs=[pl.BlockSpec((tm, N_HEADS * LANES), row),
                   pl.BlockSpec((None, tm // tk, MIX_W, tk),
                                lambda i: (i // per_seq, i % per_seq, 0, 0))],
        out_shape=[jax.ShapeDtypeStruct((t, N_HEADS * LANES), BF16),
                   jax.ShapeDtypeStruct((b, s // tk, MIX_W, tk), BF16)],
        compiler_params=_cparams("parallel"),
        name="mla_kv_prep",
    )(c_kv, kv_norm.reshape(1, -1), wk, wvt, k_rope, _pad_head_gain(g_kn), cosf, sinf)
    return q, k, vt


def _causal_attn_kernel(q_ref, k_ref, vt_ref, o_ref, *, tq, n_pairs):
    qi = pl.program_id(2)
    key = lax.broadcasted_iota(jnp.int32, (tq, tq), 0)
    qry = lax.broadcasted_iota(jnp.int32, (tq, tq), 1)
    causal = key <= qry
    scale = MLA_QK ** -0.5
    n_heads = 2 * n_pairs
    last = n_heads - 1
    qs = [q_ref[:, h * LANES:(h + 1) * LANES] for h in range(n_heads)]

    def score_tile(h, kb):
        start = pl.multiple_of(kb * tq, tq)
        return lax.dot_general(k_ref[pl.ds(start, tq), h * LANES:(h + 1) * LANES], qs[h],
                               (((1,), (1,)), ((), ())), preferred_element_type=F32)

    def softmax_step(sc, stat, diagonal):
        m, l = stat
        if diagonal:
            sc = jnp.where(causal, sc, -jnp.inf)
        m_new = jnp.maximum(m, jnp.max(sc, axis=0, keepdims=True))
        alpha = jnp.exp2((m - m_new) * (scale * LOG2_E))
        p = jnp.exp2((sc - m_new) * (scale * LOG2_E))
        return (m_new, alpha * l + jnp.sum(p, axis=0, keepdims=True)), alpha, p.astype(BF16)

    def value_step(h, kb, p, alpha, acc):
        vt = vt_ref[kb, h * HEAD_DIM:(h + 1) * HEAD_DIM, :]
        return alpha * acc + jnp.dot(vt, p, preferred_element_type=F32)

    def trip(kb, carry, diagonal):
        sc_first, p_last, alpha_last, stats, accs = carry
        accs = list(accs)
        accs[last] = value_step(last, jnp.maximum(kb - 1, 0), p_last, alpha_last, accs[last])
        scs = [sc_first] + [score_tile(h, kb) for h in range(1, n_heads)]
        sc_next = None if diagonal else score_tile(0, kb + 1)
        stats = list(stats)
        for h in range(n_heads):
            stats[h], alpha, p = softmax_step(scs[h], stats[h], diagonal)
            if h < last or diagonal:
                accs[h] = value_step(h, kb, p, alpha, accs[h])
        return sc_next, p, alpha, tuple(stats), tuple(accs)

    carry = (score_tile(0, 0),
             jnp.zeros((tq, tq), BF16),
             jnp.ones((1, tq), F32),
             ((jnp.full((1, tq), -jnp.inf, F32), jnp.zeros((1, tq), F32)),) * n_heads,
             (jnp.zeros((HEAD_DIM, tq), F32),) * n_heads)
    carry = lax.fori_loop(0, qi, lambda kb, c: trip(kb, c, False), carry)
    _, _, _, stats, accs = trip(qi, carry, True)
    outs = [accs[h] / stats[h][1] for h in range(n_heads)]
    for p in range(n_pairs):
        pair = jnp.concatenate([outs[2 * p], outs[2 * p + 1]], axis=0)
        o_ref[:, p * LANES:(p + 1) * LANES] = pair.T.astype(o_ref.dtype)


def _causal_attention(q, k, vt, b, s, tq, n_pairs):
    kern = functools.partial(_causal_attn_kernel, tq=tq, n_pairs=n_pairs)
    wq = 2 * n_pairs * LANES
    wv = n_pairs * LANES
    return pl.pallas_call(
        kern,
        grid=(b, N_PAIRS // n_pairs, s // tq),
        in_specs=[
            pl.BlockSpec((None, tq, wq), lambda bi, g, qi: (bi, qi, g)),
            pl.BlockSpec((None, s, wq), lambda bi, g, qi: (bi, 0, g)),
            pl.BlockSpec((None, s // tq, wv, tq), lambda bi, g, qi: (bi, 0, g, 0)),
        ],
        out_specs=pl.BlockSpec((None, tq, wv), lambda bi, g, qi: (bi, qi, g)),
        out_shape=jax.ShapeDtypeStruct((b, s, MIX_W), BF16),
        compiler_params=_cparams("parallel", "parallel", "arbitrary"),
        name="causal_attention",
    )(q, k, vt)


def _mem_kv_kernel(kv_ref, g_ref, k_ref, v_ref):
    k = kv_ref[:, :MEM_Q]
    v = kv_ref[:, MEM_Q:]
    g = g_ref[...]
    lane = lax.broadcasted_iota(jnp.int32, k.shape, 1)
    for h in range(N_MEM_HEADS):
        in_head = (lane >= h * HEAD_DIM) & (lane < (h + 1) * HEAD_DIM)
        ms = jnp.sum(jnp.where(in_head, k * k, 0.0), axis=-1, keepdims=True) / HEAD_DIM
        k_ref[h] = jnp.where(in_head, k * lax.rsqrt(ms + EPS) * g, 0.0).astype(k_ref.dtype)
        v_ref[h] = jnp.where(in_head, v, 0.0).astype(v_ref.dtype)


def _mem_out_kernel(mix_ref, qm_ref, k_ref, v_ref, g_ref, w_ref, x_ref, o_ref):
    q = qm_ref[...]
    lane = lax.broadcasted_iota(jnp.int32, q.shape, 1)
    inv = jnp.zeros_like(q)
    for h in range(N_MEM_HEADS):
        in_head = (lane >= h * HEAD_DIM) & (lane < (h + 1) * HEAD_DIM)
        ms = jnp.sum(jnp.where(in_head, q * q, 0.0), axis=-1, keepdims=True) / HEAD_DIM
        inv = jnp.where(in_head, lax.rsqrt(ms + EPS), inv)
    qn = (q * inv * g_ref[...]).astype(BF16)
    mem_o = jnp.zeros(q.shape, F32)
    for h in range(N_MEM_HEADS):
        sc = lax.dot_general(qn, k_ref[h], (((1,), (1,)), ((), ())), preferred_element_type=F32)
        sc = sc * (HEAD_DIM ** -0.5)
        e = jnp.exp(sc - jnp.max(sc, axis=-1, keepdims=True))
        p = e / jnp.sum(e, axis=-1, keepdims=True)
        mem_o = mem_o + jnp.dot(p.astype(BF16), v_ref[h], preferred_element_type=F32)
    out = jnp.dot(mix_ref[...], w_ref[:MIX_W, :], preferred_element_type=F32)
    out = out + jnp.dot(mem_o.astype(BF16), w_ref[MIX_W:, :], preferred_element_type=F32)
    o_ref[...] = x_ref[...] + out


def _mem_kv_prep(mem2d, ln_g, w_kv, g_kn, b, n_mem):
    w = w_kv.reshape(D_MODEL, N_MEM_HEADS, 2, HEAD_DIM).transpose(0, 2, 1, 3)
    w = w.reshape(D_MODEL, 2 * MEM_Q).astype(BF16)
    (kv,) = _rms_matmul(mem2d, ln_g, w, [(0, 2 * MEM_Q)], [F32], tm=n_mem, name="mem_kv_proj")
    shape = jax.ShapeDtypeStruct((b, N_MEM_HEADS, n_mem, MEM_Q), BF16)
    spec = pl.BlockSpec((None, N_MEM_HEADS, n_mem, MEM_Q), lambda i: (i, 0, 0, 0))
    return pl.pallas_call(
        _mem_kv_kernel,
        grid=(b,),
        in_specs=[pl.BlockSpec((n_mem, 2 * MEM_Q), lambda i: (i, 0)),
                  pl.BlockSpec((1, MEM_Q), lambda i: (0, 0))],
        out_specs=[spec, spec],
        out_shape=[shape, shape],
        compiler_params=_cparams("parallel"),
        name="mem_kv_prep",
    )(kv, jnp.tile(g_kn.astype(F32), N_MEM_HEADS).reshape(1, MEM_Q))


def _mem_out_proj(mix, q_mem, k_mem, v_mem, g_qn, w_out, x, s, tm):
    t = x.shape[0]
    n_mem = k_mem.shape[2]
    per_seq = s // tm
    row = lambda i: (i, 0)
    mem_spec = pl.BlockSpec((None, N_MEM_HEADS, n_mem, MEM_Q), lambda i: (i // per_seq, 0, 0, 0))
    return pl.pallas_call(
        _mem_out_kernel,
        grid=(t // tm,),
        in_specs=[
            pl.BlockSpec((tm, MIX_W), row),
            pl.BlockSpec((tm, MEM_Q), row),
            mem_spec,
            mem_spec,
            pl.BlockSpec((1, MEM_Q), lambda i: (0, 0)),
            pl.BlockSpec((D_MODEL, D_MODEL), lambda i: (0, 0)),
            pl.BlockSpec((tm, D_MODEL), row),
        ],
        out_specs=pl.BlockSpec((tm, D_MODEL), row),
        out_shape=jax.ShapeDtypeStruct((t, D_MODEL), F32),
        compiler_params=_cparams("parallel"),
        name="mem_attn_out_proj",
    )(mix, q_mem, k_mem, v_mem, jnp.tile(g_qn.astype(F32), N_MEM_HEADS).reshape(1, MEM_Q),
      w_out, x)


def _ffn_kernel(x_ref, g_ref, wg_ref, wu_ref, wd_ref, o_ref, h_ref, acc_ref):
    j = pl.program_id(1)

    @pl.when(j == 0)
    def _():
        h_ref[...] = _rms_rows(x_ref[...], g_ref[...]).astype(BF16)
        acc_ref[...] = x_ref[...]

    h = h_ref[...]
    gate = jnp.dot(h, wg_ref[...], preferred_element_type=F32)
    up = jnp.dot(h, wu_ref[...], preferred_element_type=F32)
    act = gate * (1.0 / (1.0 + jnp.exp(-gate))) * up
    acc_ref[...] += jnp.dot(act.astype(BF16), wd_ref[...], preferred_element_type=F32)

    @pl.when(j == pl.num_programs(1) - 1)
    def _():
        o_ref[...] = acc_ref[...]


def _ffn(x, g, w_gu, w_down, tm, tf):
    t = x.shape[0]
    n_f = D_FF // tf
    return pl.pallas_call(
        _ffn_kernel,
        grid=(t // tm, n_f),
        in_specs=[
            pl.BlockSpec((tm, D_MODEL), lambda i, j: (i, 0)),
            pl.BlockSpec((1, D_MODEL), lambda i, j: (0, 0)),
            pl.BlockSpec((D_MODEL, tf), lambda i, j: (0, j)),
            pl.BlockSpec((D_MODEL, tf), lambda i, j: (0, n_f + j)),
            pl.BlockSpec((tf, D_MODEL), lambda i, j: (j, 0)),
        ],
        out_specs=pl.BlockSpec((tm, D_MODEL), lambda i, j: (i, 0)),
        out_shape=jax.ShapeDtypeStruct((t, D_MODEL), F32),
        scratch_shapes=[pltpu.VMEM((tm, D_MODEL), BF16), pltpu.VMEM((tm, D_MODEL), F32)],
        compiler_params=_cparams("parallel", "arbitrary"),
        name="swiglu_ffn",
    )(x, g.reshape(1, D_MODEL), w_gu, w_gu, w_down)


def kernel(x, mem, positions, ln_attn, w_out, ln_mem, w_mem_kv, g_qn_mem, g_kn_mem, ln_ffn,
           w_ffn_gu, w_ffn_down, sb_w_in, mla_w_in, mla_q_norm, mla_w_q_up, mla_kv_norm,
           mla_w_kv_up, mla_g_qn, mla_g_kn):
    b, s, d = x.shape
    n_mem = mem.shape[1]
    t = b * s
    x2 = x.reshape(t, d)
    mem2 = mem.reshape(b * n_mem, d)

    qk, q_mem, vt = _sb_in_proj(x2, ln_attn[0], sb_w_in[0], b, s, tm=512, tk=256)
    mix = _sb_attention(qk.reshape(b, s, 2 * MIX_W), vt, b, s, tq=256, n_pairs=2).reshape(t, MIX_W)
    k_mem, v_mem = _mem_kv_prep(mem2, ln_mem[0], w_mem_kv[0], g_kn_mem[0], b, n_mem)
    x2 = _mem_out_proj(mix, q_mem, k_mem, v_mem, g_qn_mem[0], w_out[0].astype(BF16), x2, s, tm=512)
    x2 = _ffn(x2, ln_ffn[0], w_ffn_gu[0].astype(BF16), w_ffn_down[0].astype(BF16), tm=1024, tf=256)

    w_in = mla_w_in[0]
    o_kr = MLA_Q_RANK + MLA_KV_RANK
    o_qm = o_kr + MLA_ROPE
    w_kr = jnp.pad(w_in[:, o_kr:o_qm], ((0, 0), (MLA_NOPE, LANES - MLA_QK)))
    w_in = jnp.concatenate([w_in[:, :o_kr], w_in[:, o_qm:], w_kr], axis=1).astype(BF16)
    c_q, c_kv, q_mem, k_rope = _rms_matmul(
        x2, ln_attn[1], w_in,
        [(0, MLA_Q_RANK), (MLA_Q_RANK, MLA_KV_RANK), (o_kr, MEM_Q), (o_kr + MEM_Q, LANES)],
        [F32, F32, F32, F32], tm=512, name="mla_in_proj")
    cosf, sinf = _rope_tables(positions, tm=512)
    q, k, vt = _mla_prep(c_q, c_kv, k_rope, cosf, sinf, mla_q_norm[0], mla_w_q_up[0],
                         mla_kv_norm[0], mla_w_kv_up[0], mla_g_qn[0], mla_g_kn[0], b, s,
                         tm=512, tk=256)
    mix = _causal_attention(q.reshape(b, s, -1), k.reshape(b, s, -1), vt,
                            b, s, tq=256, n_pairs=2).reshape(t, MIX_W)
    k_mem, v_mem = _mem_kv_prep(mem2, ln_mem[1], w_mem_kv[1], g_kn_mem[1], b, n_mem)
    x2 = _mem_out_proj(mix, q_mem, k_mem, v_mem, g_qn_mem[1], w_out[1].astype(BF16), x2, s, tm=512)
    x2 = _ffn(x2, ln_ffn[1], w_ffn_gu[1].astype(BF16), w_ffn_down[1].astype(BF16), tm=1024, tf=256)
    return x2.reshape(b, s, d)
```

```python
import functools

import jax
import jax.numpy as jnp
from jax import lax
from jax.experimental import pallas as pl
from jax.experimental.pallas import tpu as pltpu

F32 = jnp.float32
BF16 = jnp.bfloat16

D_MODEL = 1024
HEAD_DIM = 64
N_HEADS = 12
N_PAIRS = N_HEADS // 2
MIX_W = N_HEADS * HEAD_DIM
MLA_Q_RANK = 768
MLA_KV_RANK = 256
MLA_NOPE = 64
MLA_ROPE = 32
MLA_QK = MLA_NOPE + MLA_ROPE
N_MEM_HEADS = 4
MEM_Q = N_MEM_HEADS * HEAD_DIM
D_FF = 2816
ROPE_THETA = 10000.0
EPS = 1e-6
LOG2_E = 1.4426950408889634
LANES = 128

VMEM_LIMIT = 56 * 1024 * 1024


def _cparams(*sem):
    return pltpu.CompilerParams(dimension_semantics=sem, vmem_limit_bytes=VMEM_LIMIT)


def _rms_rows(x, g):
    ms = jnp.mean(x * x, axis=-1, keepdims=True)
    return x * lax.rsqrt(ms + EPS) * g


def _rms_matmul_kernel(x_ref, g_ref, w_ref, *o_refs, segments, chunk):
    h = _rms_rows(x_ref[...].astype(F32), g_ref[...]).astype(BF16)
    for o_ref, (start, width) in zip(o_refs, segments):
        for c in range(0, width, chunk):
            cw = min(chunk, width - c)
            acc = jnp.dot(h, w_ref[:, start + c:start + c + cw], preferred_element_type=F32)
            o_ref[:, c:c + cw] = acc.astype(o_ref.dtype)


def _rms_matmul(x, g, w, segments, dtypes, tm, name):
    t, k = x.shape
    n = w.shape[1]
    kern = functools.partial(_rms_matmul_kernel, segments=tuple(segments), chunk=512)
    return pl.pallas_call(
        kern,
        grid=(t // tm,),
        in_specs=[
            pl.BlockSpec((tm, k), lambda i: (i, 0)),
            pl.BlockSpec((1, k), lambda i: (0, 0)),
            pl.BlockSpec((k, n), lambda i: (0, 0)),
        ],
        out_specs=[pl.BlockSpec((tm, wd), lambda i: (i, 0)) for _, wd in segments],
        out_shape=[jax.ShapeDtypeStruct((t, wd), dt) for (_, wd), dt in zip(segments, dtypes)],
        compiler_params=_cparams("parallel"),
        name=name,
    )(x, g.reshape(1, k), w)


N_SPLIT = 2
U_EXTRA = 16


def _sb_in_proj_kernel(x_ref, g_ref, w_ref, wvt_ref, qk_ref, qm_ref, vt_ref):
    h = _rms_rows(x_ref[...], g_ref[...]).astype(BF16)
    n_qk = qk_ref.shape[1]
    for c in range(0, n_qk, 512):
        qk_ref[:, c:c + 512] = jnp.dot(h, w_ref[:, c:c + 512],
                                       preferred_element_type=F32).astype(qk_ref.dtype)
    qm_ref[...] = jnp.dot(h, w_ref[:, n_qk:], preferred_element_type=F32)
    vt = lax.dot_general(wvt_ref[...], h, (((1,), (1,)), ((), ())), preferred_element_type=F32)
    tk = vt_ref.shape[-1]
    for c in range(vt_ref.shape[0]):
        vt_ref[c] = vt[:, c * tk:(c + 1) * tk].astype(vt_ref.dtype)


def _sb_in_proj(x, g, w_in, b, s, tm, tk):
    t = x.shape[0]
    w_qk_qm = jnp.concatenate([w_in[:, :2 * MIX_W], w_in[:, 3 * MIX_W:]], axis=1).astype(BF16)
    wvt = w_in[:, 2 * MIX_W:3 * MIX_W].T.astype(BF16)
    per_seq = s // tm
    row = lambda i: (i, 0)
    fixed = lambda i: (0, 0)
    return pl.pallas_call(
        _sb_in_proj_kernel,
        grid=(t // tm,),
        in_specs=[
            pl.BlockSpec((tm, D_MODEL), row),
            pl.BlockSpec((1, D_MODEL), fixed),
            pl.BlockSpec((D_MODEL, 2 * MIX_W + MEM_Q), fixed),
            pl.BlockSpec((MIX_W, D_MODEL), fixed),
        ],
        out_specs=[pl.BlockSpec((tm, 2 * MIX_W), row), pl.BlockSpec((tm, MEM_Q), row),
                   pl.BlockSpec((None, tm // tk, MIX_W, tk),
                                lambda i: (i // per_seq, i % per_seq, 0, 0))],
        out_shape=[jax.ShapeDtypeStruct((t, 2 * MIX_W), BF16),
                   jax.ShapeDtypeStruct((t, MEM_Q), F32),
                   jax.ShapeDtypeStruct((b, s // tk, MIX_W, tk), BF16)],
        compiler_params=_cparams("parallel"),
        name="sb_in_proj",
    )(x, g.reshape(1, D_MODEL), w_qk_qm, wvt)


def _split_rows(x):
    parts = []
    r = x
    for i in range(N_SPLIT):
        piece = r.astype(BF16)
        parts.append(piece)
        if i + 1 < N_SPLIT:
            r = r - piece.astype(F32)
    return jnp.concatenate(parts, axis=0)


def _sb_attn_kernel(q_ref, k_ref, vt_ref, u_ref, o_ref, *, tq, n_pairs):
    qi = pl.program_id(2)
    lane = lax.broadcasted_iota(jnp.int32, (tq, LANES), 1)
    key = lax.broadcasted_iota(jnp.int32, (tq, tq), 0)
    qry = lax.broadcasted_iota(jnp.int32, (tq, tq), 1)
    strict = key < qry
    u = u_ref[...]
    heads = [(p, h) for p in range(n_pairs) for h in range(2)]
    n = len(heads)
    qs = []
    for p, h in heads:
        in_head = (lane >= h * HEAD_DIM) & (lane < (h + 1) * HEAD_DIM)
        q_pair = q_ref[:, p * LANES:(p + 1) * LANES]
        qs.append(jnp.where(in_head, q_pair * (HEAD_DIM ** -0.5), 0).astype(BF16))

    def block(kb, state, diagonal):
        start = pl.multiple_of(kb * tq, tq)
        zs = []
        for idx in range(n):
            p = heads[idx][0]
            k = k_ref[pl.ds(start, tq), p * LANES:(p + 1) * LANES]
            zs.append(lax.dot_general(k, qs[idx], (((1,), (1,)), ((), ())),
                                      preferred_element_type=F32))
        log_betas, log_fails = [], []
        for z in zs:
            soft = jnp.log(1.0 + jnp.exp(-jnp.abs(z)))
            log_beta = jnp.minimum(z, 0.0) - soft
            log_fail = log_beta - z
            if diagonal:
                log_fail = jnp.where(strict, log_fail, 0.0)
            log_betas.append(log_beta)
            log_fails.append(log_fail)
        sums = [jnp.dot(u, _split_rows(lf), preferred_element_type=F32) for lf in log_fails]
        ws = []
        for idx in range(n):
            w = jnp.exp(log_betas[idx] + sums[idx][:tq] + state[idx][0])
            if diagonal:
                w = jnp.where(strict, w, 0.0)
            ws.append(w.astype(BF16))
        new = []
        for idx in range(n):
            vt = vt_ref[kb, idx * HEAD_DIM:(idx + 1) * HEAD_DIM, :]
            acc = state[idx][1] + jnp.dot(vt, ws[idx], preferred_element_type=F32)
            carry = state[idx][0] + sums[idx][tq:tq + 1]
            new.append((carry, acc))
        return tuple(new)

    zero = (jnp.zeros((1, tq), F32), jnp.zeros((HEAD_DIM, tq), F32))
    state = block(qi, (zero,) * n, True)
    state = lax.fori_loop(0, qi, lambda i, st: block(qi - 1 - i, st, False), state)
    for p in range(n_pairs):
        pair = jnp.concatenate([state[2 * p][1], state[2 * p + 1][1]], axis=0)
        o_ref[:, p * LANES:(p + 1) * LANES] = pair.T.astype(o_ref.dtype)


def _sb_attention(qk, vt, b, s, tq, n_pairs):
    r = jnp.arange(tq)
    u = (r[None, :] > r[:, None]).astype(BF16)
    u = jnp.concatenate([u, jnp.ones((U_EXTRA, tq), BF16)], axis=0)
    u = jnp.concatenate([u] * N_SPLIT, axis=1)
    kern = functools.partial(_sb_attn_kernel, tq=tq, n_pairs=n_pairs)
    w = n_pairs * LANES
    groups = N_PAIRS // n_pairs
    return pl.pallas_call(
        kern,
        grid=(b, groups, s // tq),
        in_specs=[
            pl.BlockSpec((None, tq, w), lambda bi, g, qi: (bi, qi, g)),
            pl.BlockSpec((None, s, w), lambda bi, g, qi: (bi, 0, groups + g)),
            pl.BlockSpec((None, s // tq, w, tq), lambda bi, g, qi: (bi, 0, g, 0)),
            pl.BlockSpec((tq + U_EXTRA, N_SPLIT * tq), lambda bi, g, qi: (0, 0)),
        ],
        out_specs=pl.BlockSpec((None, tq, w), lambda bi, g, qi: (bi, qi, g)),
        out_shape=jax.ShapeDtypeStruct((b, s, MIX_W), BF16),
        compiler_params=_cparams("parallel", "parallel", "arbitrary"),
        name="sb_attention",
    )(qk, qk, vt, u)


def _rope_table_kernel(pos_ref, invf_ref, sign_ref, cos_ref, sin_ref):
    ang = pos_ref[...].astype(F32) * invf_ref[...]
    cos_ref[...] = jnp.cos(ang)
    sin_ref[...] = jnp.sin(ang) * sign_ref[...]


def _rope_tables(positions, tm):
    t = positions.size
    half = MLA_ROPE // 2
    inv_freq = ROPE_THETA ** (-jnp.arange(half, dtype=F32) / half)
    zeros = jnp.zeros((MLA_NOPE,), F32)
    tail = jnp.zeros((LANES - MLA_QK,), F32)
    invf = jnp.concatenate([zeros, inv_freq, inv_freq, tail]).reshape(1, LANES)
    ones = jnp.ones((half,), F32)
    sign = jnp.concatenate([zeros, -ones, ones, tail]).reshape(1, LANES)
    return pl.pallas_call(
        _rope_table_kernel,
        grid=(t // tm,),
        in_specs=[
            pl.BlockSpec((tm, 1), lambda i: (i, 0)),
            pl.BlockSpec((1, LANES), lambda i: (0, 0)),
            pl.BlockSpec((1, LANES), lambda i: (0, 0)),
        ],
        out_specs=[pl.BlockSpec((tm, LANES), lambda i: (i, 0))] * 2,
        out_shape=[jax.ShapeDtypeStruct((t, LANES), F32)] * 2,
        compiler_params=_cparams("parallel"),
        name="rope_tables",
    )(positions.reshape(t, 1), invf, sign)


def _head_norm_rope(x, g, cosf, sinf):
    ms = jnp.sum(x * x, axis=-1, keepdims=True) / MLA_QK
    y = x * lax.rsqrt(ms + EPS) * g
    lane = lax.broadcasted_iota(jnp.int32, y.shape, 1)
    half = MLA_ROPE // 2
    partner = jnp.where(lane < MLA_NOPE + half,
                        pltpu.roll(y, LANES - half, 1), pltpu.roll(y, half, 1))
    return y * cosf + partner * sinf


def _mla_q_kernel(cq_ref, gq_ref, w_ref, gh_ref, cos_ref, sin_ref, q_ref):
    h = _rms_rows(cq_ref[...], gq_ref[...]).astype(BF16)
    cosf = cos_ref[...]
    sinf = sin_ref[...]
    g = gh_ref[...]
    for hd in range(N_HEADS):
        sl = slice(hd * LANES, (hd + 1) * LANES)
        x = jnp.dot(h, w_ref[:, sl], preferred_element_type=F32)
        q_ref[:, sl] = _head_norm_rope(x, g, cosf, sinf).astype(q_ref.dtype)


def _mla_k_kernel(ckv_ref, gkv_ref, w_ref, wvt_ref, kr_ref, gh_ref, cos_ref, sin_ref,
                  k_ref, vt_ref):
    h = _rms_rows(ckv_ref[...], gkv_ref[...]).astype(BF16)
    cosf = cos_ref[...]
    sinf = sin_ref[...]
    g = gh_ref[...]
    k_rope = kr_ref[...]
    for hd in range(N_HEADS):
        sl = slice(hd * LANES, (hd + 1) * LANES)
        x = jnp.dot(h, w_ref[:, sl], preferred_element_type=F32) + k_rope
        k_ref[:, sl] = _head_norm_rope(x, g, cosf, sinf).astype(k_ref.dtype)
    vt = lax.dot_general(wvt_ref[...], h, (((1,), (1,)), ((), ())), preferred_element_type=F32)
    tk = vt_ref.shape[-1]
    for c in range(vt_ref.shape[0]):
        vt_ref[c] = vt[:, c * tk:(c + 1) * tk].astype(vt_ref.dtype)


def _pad_head_gain(g):
    return jnp.concatenate([g.astype(F32), jnp.zeros((LANES - MLA_QK,), F32)]).reshape(1, LANES)


def _mla_prep(c_q, c_kv, k_rope, cosf, sinf, q_norm, w_q_up, kv_norm, w_kv_up, g_qn, g_kn,
              b, s, tm, tk):
    t = c_q.shape[0]
    pad = LANES - MLA_QK
    wq = jnp.pad(w_q_up.reshape(MLA_Q_RANK, N_HEADS, MLA_QK), ((0, 0), (0, 0), (0, pad)))
    wq = wq.reshape(MLA_Q_RANK, N_HEADS * LANES).astype(BF16)
    wkv = w_kv_up.reshape(MLA_KV_RANK, N_HEADS, MLA_NOPE + HEAD_DIM)
    wk = jnp.pad(wkv[:, :, :MLA_NOPE], ((0, 0), (0, 0), (0, LANES - MLA_NOPE)))
    wk = wk.reshape(MLA_KV_RANK, N_HEADS * LANES).astype(BF16)
    wvt = wkv[:, :, MLA_NOPE:].reshape(MLA_KV_RANK, MIX_W).T.astype(BF16)

    row = lambda i: (i, 0)
    fixed = lambda i: (0, 0)
    q = pl.pallas_call(
        _mla_q_kernel,
        grid=(t // tm,),
        in_specs=[
            pl.BlockSpec((tm, MLA_Q_RANK), row),
            pl.BlockSpec((1, MLA_Q_RANK), fixed),
            pl.BlockSpec((MLA_Q_RANK, N_HEADS * LANES), fixed),
            pl.BlockSpec((1, LANES), fixed),
            pl.BlockSpec((tm, LANES), row),
            pl.BlockSpec((tm, LANES), row),
        ],
        out_specs=pl.BlockSpec((tm, N_HEADS * LANES), row),
        out_shape=jax.ShapeDtypeStruct((t, N_HEADS * LANES), BF16),
        compiler_params=_cparams("parallel"),
        name="mla_q_prep",
    )(c_q, q_norm.reshape(1, -1), wq, _pad_head_gain(g_qn), cosf, sinf)
    per_seq = s // tm
    k, vt = pl.pallas_call(
        _mla_k_kernel,
        grid=(t // tm,),
        in_specs=[
            pl.BlockSpec((tm, MLA_KV_RANK), row),
            pl.BlockSpec((1, MLA_KV_RANK), fixed),
            pl.BlockSpec((MLA_KV_RANK, N_HEADS * LANES), fixed),
            pl.BlockSpec((MIX_W, MLA_KV_RANK), fixed),
            pl.BlockSpec((tm, LANES), row),
            pl.BlockSpec((1, LANES), fixed),
            pl.BlockSpec((tm, LANES), row),
            pl.BlockSpec((tm, LANES), row),
        ],
        out_specs=[pl.BlockSpec((tm, N_HEADS * LANES), row),
                   pl.BlockSpec((None, tm // tk, MIX_W, tk),
                                lambda i: (i // per_seq, i % per_seq, 0, 0))],
        out_shape=[jax.ShapeDtypeStruct((t, N_HEADS * LANES), BF16),
                   jax.ShapeDtypeStruct((b, s // tk, MIX_W, tk), BF16)],
        compiler_params=_cparams("parallel"),
        name="mla_kv_prep",
    )(c_kv, kv_norm.reshape(1, -1), wk, wvt, k_rope, _pad_head_gain(g_kn), cosf, sinf)
    return q, k, vt


def _causal_attn_kernel(q_ref, k_ref, vt_ref, o_ref, sc_ref, *, tq, n_pairs):
    qi = pl.program_id(2)
    key = lax.broadcasted_iota(jnp.int32, (tq, tq), 0)
    qry = lax.broadcasted_iota(jnp.int32, (tq, tq), 1)
    causal = key <= qry
    c = MLA_QK ** -0.5 * LOG2_E
    n_heads = 2 * n_pairs
    qs = [q_ref[:, h * LANES:(h + 1) * LANES] for h in range(n_heads)]

    def score_tiles(kb):
        start = pl.multiple_of(kb * tq, tq)
        return [lax.dot_general(k_ref[pl.ds(start, tq), h * LANES:(h + 1) * LANES], qs[h],
                                (((1,), (1,)), ((), ())), preferred_element_type=F32)
                for h in range(n_heads)]

    def park(scs, ms, diagonal):
        new_ms, alphas = [], []
        for h in range(n_heads):
            sc = jnp.where(causal, scs[h], -jnp.inf) if diagonal else scs[h]
            m_new = jnp.maximum(ms[h], jnp.max(sc, axis=0, keepdims=True))
            alphas.append(jnp.exp2((ms[h] - m_new) * c))
            new_ms.append(m_new)
            sc_ref[h] = sc
        return tuple(new_ms), tuple(alphas)

    def consume(kb, ms, alphas, ls, accs):
        new_ls, new_accs = [], []
        for h in range(n_heads):
            p = jnp.exp2((sc_ref[h] - ms[h]) * c)
            new_ls.append(alphas[h] * ls[h] + jnp.sum(p, axis=0, keepdims=True))
            vt = vt_ref[kb, h * HEAD_DIM:(h + 1) * HEAD_DIM, :]
            new_accs.append(alphas[h] * accs[h]
                            + jnp.dot(vt, p.astype(BF16), preferred_element_type=F32))
        return tuple(new_ls), tuple(new_accs)

    row = lambda v: (jnp.full((1, tq), v, F32),) * n_heads
    ms, alphas = park(score_tiles(qi), row(-jnp.inf), True)
    ls = row(0.0)
    accs = (jnp.zeros((HEAD_DIM, tq), F32),) * n_heads

    def trip(j, carry):
        ms, alphas, ls, accs = carry
        scs = score_tiles(j)
        ls, accs = consume(jnp.where(j == 0, qi, j - 1), ms, alphas, ls, accs)
        ms, alphas = park(scs, ms, False)
        return ms, alphas, ls, accs

    ms, alphas, ls, accs = lax.fori_loop(0, qi, trip, (ms, alphas, ls, accs))
    ls, accs = consume(jnp.where(qi == 0, qi, qi - 1), ms, alphas, ls, accs)
    outs = [accs[h] / ls[h] for h in range(n_heads)]
    for p in range(n_pairs):
        pair = jnp.concatenate([outs[2 * p], outs[2 * p + 1]], axis=0)
        o_ref[:, p * LANES:(p + 1) * LANES] = pair.T.astype(o_ref.dtype)


def _causal_attention(q, k, vt, b, s, tq, n_pairs):
    kern = functools.partial(_causal_attn_kernel, tq=tq, n_pairs=n_pairs)
    wq = 2 * n_pairs * LANES
    wv = n_pairs * LANES
    return pl.pallas_call(
        kern,
        grid=(b, N_PAIRS // n_pairs, s // tq),
        in_specs=[
            pl.BlockSpec((None, tq, wq), lambda bi, g, qi: (bi, qi, g)),
            pl.BlockSpec((None, s, wq), lambda bi, g, qi: (bi, 0, g)),
            pl.BlockSpec((None, s // tq, wv, tq), lambda bi, g, qi: (bi, 0, g, 0)),
        ],
        out_specs=pl.BlockSpec((None, tq, wv), lambda bi, g, qi: (bi, qi, g)),
        out_shape=jax.ShapeDtypeStruct((b, s, MIX_W), BF16),
        scratch_shapes=[pltpu.VMEM((2 * n_pairs, tq, tq), F32)],
        compiler_params=_cparams("parallel", "parallel", "arbitrary"),
        name="causal_attention",
    )(q, k, vt)


def _mem_kv_kernel(kv_ref, g_ref, k_ref, v_ref):
    k = kv_ref[:, :MEM_Q]
    v = kv_ref[:, MEM_Q:]
    g = g_ref[...]
    lane = lax.broadcasted_iota(jnp.int32, k.shape, 1)
    for h in range(N_MEM_HEADS):
        in_head = (lane >= h * HEAD_DIM) & (lane < (h + 1) * HEAD_DIM)
        ms = jnp.sum(jnp.where(in_head, k * k, 0.0), axis=-1, keepdims=True) / HEAD_DIM
        k_ref[h] = jnp.where(in_head, k * lax.rsqrt(ms + EPS) * g, 0.0).astype(k_ref.dtype)
        v_ref[h] = jnp.where(in_head, v, 0.0).astype(v_ref.dtype)


def _mem_out_kernel(mix_ref, qm_ref, k_ref, v_ref, g_ref, w_ref, x_ref, o_ref):
    q = qm_ref[...]
    lane = lax.broadcasted_iota(jnp.int32, q.shape, 1)
    inv = jnp.zeros_like(q)
    for h in range(N_MEM_HEADS):
        in_head = (lane >= h * HEAD_DIM) & (lane < (h + 1) * HEAD_DIM)
        ms = jnp.sum(jnp.where(in_head, q * q, 0.0), axis=-1, keepdims=True) / HEAD_DIM
        inv = jnp.where(in_head, lax.rsqrt(ms + EPS), inv)
    qn = (q * inv * g_ref[...]).astype(BF16)
    mem_o = jnp.zeros(q.shape, F32)
    for h in range(N_MEM_HEADS):
        sc = lax.dot_general(qn, k_ref[h], (((1,), (1,)), ((), ())), preferred_element_type=F32)
        sc = sc * (HEAD_DIM ** -0.5)
        e = jnp.exp(sc - jnp.max(sc, axis=-1, keepdims=True))
        p = e / jnp.sum(e, axis=-1, keepdims=True)
        mem_o = mem_o + jnp.dot(p.astype(BF16), v_ref[h], preferred_element_type=F32)
    out = jnp.dot(mix_ref[...], w_ref[:MIX_W, :], preferred_element_type=F32)
    out = out + jnp.dot(mem_o.astype(BF16), w_ref[MIX_W:, :], preferred_element_type=F32)
    o_ref[...] = x_ref[...] + out


def _mem_kv_prep(mem2d, ln_g, w_kv, g_kn, b, n_mem):
    w = w_kv.reshape(D_MODEL, N_MEM_HEADS, 2, HEAD_DIM).transpose(0, 2, 1, 3)
    w = w.reshape(D_MODEL, 2 * MEM_Q).astype(BF16)
    (kv,) = _rms_matmul(mem2d, ln_g, w, [(0, 2 * MEM_Q)], [F32], tm=n_mem, name="mem_kv_proj")
    shape = jax.ShapeDtypeStruct((b, N_MEM_HEADS, n_mem, MEM_Q), BF16)
    spec = pl.BlockSpec((None, N_MEM_HEADS, n_mem, MEM_Q), lambda i: (i, 0, 0, 0))
    return pl.pallas_call(
        _mem_kv_kernel,
        grid=(b,),
        in_specs=[pl.BlockSpec((n_mem, 2 * MEM_Q), lambda i: (i, 0)),
                  pl.BlockSpec((1, MEM_Q), lambda i: (0, 0))],
        out_specs=[spec, spec],
        out_shape=[shape, shape],
        compiler_params=_cparams("parallel"),
        name="mem_kv_prep",
    )(kv, jnp.tile(g_kn.astype(F32), N_MEM_HEADS).reshape(1, MEM_Q))


def _mem_out_proj(mix, q_mem, k_mem, v_mem, g_qn, w_out, x, s, tm):
    t = x.shape[0]
    n_mem = k_mem.shape[2]
    per_seq = s // tm
    row = lambda i: (i, 0)
    mem_spec = pl.BlockSpec((None, N_MEM_HEADS, n_mem, MEM_Q), lambda i: (i // per_seq, 0, 0, 0))
    return pl.pallas_call(
        _mem_out_kernel,
        grid=(t // tm,),
        in_specs=[
            pl.BlockSpec((tm, MIX_W), row),
            pl.BlockSpec((tm, MEM_Q), row),
            mem_spec,
            mem_spec,
            pl.BlockSpec((1, MEM_Q), lambda i: (0, 0)),
            pl.BlockSpec((D_MODEL, D_MODEL), lambda i: (0, 0)),
            pl.BlockSpec((tm, D_MODEL), row),
        ],
        out_specs=pl.BlockSpec((tm, D_MODEL), row),
        out_shape=jax.ShapeDtypeStruct((t, D_MODEL), F32),
        compiler_params=_cparams("parallel"),
        name="mem_attn_out_proj",
    )(mix, q_mem, k_mem, v_mem, jnp.tile(g_qn.astype(F32), N_MEM_HEADS).reshape(1, MEM_Q),
      w_out, x)


def _ffn_kernel(x_ref, g_ref, wg_ref, wu_ref, wd_ref, o_ref, h_ref, acc_ref):
    j = pl.program_id(1)

    @pl.when(j == 0)
    def _():
        h_ref[...] = _rms_rows(x_ref[...], g_ref[...]).astype(BF16)
        acc_ref[...] = x_ref[...]

    h = h_ref[...]
    gate = jnp.dot(h, wg_ref[...], preferred_element_type=F32)
    up = jnp.dot(h, wu_ref[...], preferred_element_type=F32)
    act = gate * (1.0 / (1.0 + jnp.exp(-gate))) * up
    acc_ref[...] += jnp.dot(act.astype(BF16), wd_ref[...], preferred_element_type=F32)

    @pl.when(j == pl.num_programs(1) - 1)
    def _():
        o_ref[...] = acc_ref[...]


def _ffn(x, g, w_gu, w_down, tm, tf):
    t = x.shape[0]
    n_f = D_FF // tf
    return pl.pallas_call(
        _ffn_kernel,
        grid=(t // tm, n_f),
        in_specs=[
            pl.BlockSpec((tm, D_MODEL), lambda i, j: (i, 0)),
            pl.BlockSpec((1, D_MODEL), lambda i, j: (0, 0)),
            pl.BlockSpec((D_MODEL, tf), lambda i, j: (0, j)),
            pl.BlockSpec((D_MODEL, tf), lambda i, j: (0, n_f + j)),
            pl.BlockSpec((tf, D_MODEL), lambda i, j: (j, 0)),
        ],
        out_specs=pl.BlockSpec((tm, D_MODEL), lambda i, j: (i, 0)),
        out_shape=jax.ShapeDtypeStruct((t, D_MODEL), F32),
        scratch_shapes=[pltpu.VMEM((tm, D_MODEL), BF16), pltpu.VMEM((tm, D_MODEL), F32)],
        compiler_params=_cparams("parallel", "arbitrary"),
        name="swiglu_ffn",
    )(x, g.reshape(1, D_MODEL), w_gu, w_gu, w_down)


def kernel(x, mem, positions, ln_attn, w_out, ln_mem, w_mem_kv, g_qn_mem, g_kn_mem, ln_ffn,
           w_ffn_gu, w_ffn_down, sb_w_in, mla_w_in, mla_q_norm, mla_w_q_up, mla_kv_norm,
           mla_w_kv_up, mla_g_qn, mla_g_kn):
    b, s, d = x.shape
    n_mem = mem.shape[1]
    t = b * s
    x2 = x.reshape(t, d)
    mem2 = mem.reshape(b * n_mem, d)

    qk, q_mem, vt = _sb_in_proj(x2, ln_attn[0], sb_w_in[0], b, s, tm=512, tk=256)
    mix = _sb_attention(qk.reshape(b, s, 2 * MIX_W), vt, b, s, tq=256, n_pairs=2).reshape(t, MIX_W)
    k_mem, v_mem = _mem_kv_prep(mem2, ln_mem[0], w_mem_kv[0], g_kn_mem[0], b, n_mem)
    x2 = _mem_out_proj(mix, q_mem, k_mem, v_mem, g_qn_mem[0], w_out[0].astype(BF16), x2, s, tm=512)
    x2 = _ffn(x2, ln_ffn[0], w_ffn_gu[0].astype(BF16), w_ffn_down[0].astype(BF16), tm=1024, tf=256)

    w_in = mla_w_in[0]
    o_kr = MLA_Q_RANK + MLA_KV_RANK
    o_qm = o_kr + MLA_ROPE
    w_kr = jnp.pad(w_in[:, o_kr:o_qm], ((0, 0), (MLA_NOPE, LANES - MLA_QK)))
    w_in = jnp.concatenate([w_in[:, :o_kr], w_in[:, o_qm:], w_kr], axis=1).astype(BF16)
    c_q, c_kv, q_mem, k_rope = _rms_matmul(
        x2, ln_attn[1], w_in,
        [(0, MLA_Q_RANK), (MLA_Q_RANK, MLA_KV_RANK), (o_kr, MEM_Q), (o_kr + MEM_Q, LANES)],
        [F32, F32, F32, F32], tm=512, name="mla_in_proj")
    cosf, sinf = _rope_tables(positions, tm=512)
    q, k, vt = _mla_prep(c_q, c_kv, k_rope, cosf, sinf, mla_q_norm[0], mla_w_q_up[0],
                         mla_kv_norm[0], mla_w_kv_up[0], mla_g_qn[0], mla_g_kn[0], b, s,
                         tm=512, tk=256)
    mix = _causal_attention(q.reshape(b, s, -1), k.reshape(b, s, -1), vt,
                            b, s, tq=256, n_pairs=2).reshape(t, MIX_W)
    k_mem, v_mem = _mem_kv_prep(mem2, ln_mem[1], w_mem_kv[1], g_kn_mem[1], b, n_mem)
    x2 = _mem_out_proj(mix, q_mem, k_mem, v_mem, g_qn_mem[1], w_out[1].astype(BF16), x2, s, tm=512)
    x2 = _ffn(x2, ln_ffn[1], w_ffn_gu[1].astype(BF16), w_ffn_down[1].astype(BF16), tm=1024, tf=256)
    return x2.reshape(b, s, d)
```

```python
import functools

import jax
import jax.numpy as jnp
from jax import lax
from jax.experimental import pallas as pl
from jax.experimental.pallas import tpu as pltpu

F32 = jnp.float32
BF16 = jnp.bfloat16

D_MODEL = 1024
HEAD_DIM = 64
N_HEADS = 12
N_PAIRS = N_HEADS // 2
MIX_W = N_HEADS * HEAD_DIM
MLA_Q_RANK = 768
MLA_KV_RANK = 256
MLA_NOPE = 64
MLA_ROPE = 32
MLA_QK = MLA_NOPE + MLA_ROPE
N_MEM_HEADS = 4
MEM_Q = N_MEM_HEADS * HEAD_DIM
D_FF = 2816
ROPE_THETA = 10000.0
EPS = 1e-6
LOG2_E = 1.4426950408889634
LANES = 128

VMEM_LIMIT = 56 * 1024 * 1024


def _cparams(*sem):
    return pltpu.CompilerParams(dimension_semantics=sem, vmem_limit_bytes=VMEM_LIMIT)


def _rms_rows(x, g):
    ms = jnp.mean(x * x, axis=-1, keepdims=True)
    return x * lax.rsqrt(ms + EPS) * g


def _rms_matmul_kernel(x_ref, g_ref, w_ref, *o_refs, segments, chunk):
    h = _rms_rows(x_ref[...].astype(F32), g_ref[...]).astype(BF16)
    for o_ref, (start, width) in zip(o_refs, segments):
        for c in range(0, width, chunk):
            cw = min(chunk, width - c)
            acc = jnp.dot(h, w_ref[:, start + c:start + c + cw], preferred_element_type=F32)
            o_ref[:, c:c + cw] = acc.astype(o_ref.dtype)


def _rms_matmul(x, g, w, segments, dtypes, tm, name):
    t, k = x.shape
    n = w.shape[1]
    kern = functools.partial(_rms_matmul_kernel, segments=tuple(segments), chunk=512)
    return pl.pallas_call(
        kern,
        grid=(t // tm,),
        in_specs=[
            pl.BlockSpec((tm, k), lambda i: (i, 0)),
            pl.BlockSpec((1, k), lambda i: (0, 0)),
            pl.BlockSpec((k, n), lambda i: (0, 0)),
        ],
        out_specs=[pl.BlockSpec((tm, wd), lambda i: (i, 0)) for _, wd in segments],
        out_shape=[jax.ShapeDtypeStruct((t, wd), dt) for (_, wd), dt in zip(segments, dtypes)],
        compiler_params=_cparams("parallel"),
        name=name,
    )(x, g.reshape(1, k), w)


N_SPLIT = 2
EXP_UNDERFLOW = -104.0
U_EXTRA = 16


def _sb_in_proj_kernel(x_ref, g_ref, w_ref, wvt_ref, qk_ref, qm_ref, vt_ref):
    h = _rms_rows(x_ref[...], g_ref[...]).astype(BF16)
    n_qk = qk_ref.shape[1]
    for c in range(0, n_qk, 512):
        qk_ref[:, c:c + 512] = jnp.dot(h, w_ref[:, c:c + 512],
                                       preferred_element_type=F32).astype(qk_ref.dtype)
    qm_ref[...] = jnp.dot(h, w_ref[:, n_qk:], preferred_element_type=F32)
    vt = lax.dot_general(wvt_ref[...], h, (((1,), (1,)), ((), ())), preferred_element_type=F32)
    tk = vt_ref.shape[-1]
    for c in range(vt_ref.shape[0]):
        vt_ref[c] = vt[:, c * tk:(c + 1) * tk].astype(vt_ref.dtype)


def _sb_in_proj(x, g, w_in, b, s, tm, tk):
    t = x.shape[0]
    w_qk_qm = jnp.concatenate([w_in[:, :2 * MIX_W], w_in[:, 3 * MIX_W:]], axis=1).astype(BF16)
    wvt = w_in[:, 2 * MIX_W:3 * MIX_W].T.astype(BF16)
    per_seq = s // tm
    row = lambda i: (i, 0)
    fixed = lambda i: (0, 0)
    return pl.pallas_call(
        _sb_in_proj_kernel,
        grid=(t // tm,),
        in_specs=[
            pl.BlockSpec((tm, D_MODEL), row),
            pl.BlockSpec((1, D_MODEL), fixed),
            pl.BlockSpec((D_MODEL, 2 * MIX_W + MEM_Q), fixed),
            pl.BlockSpec((MIX_W, D_MODEL), fixed),
        ],
        out_specs=[pl.BlockSpec((tm, 2 * MIX_W), row), pl.BlockSpec((tm, MEM_Q), row),
                   pl.BlockSpec((None, tm // tk, MIX_W, tk),
                                lambda i: (i // per_seq, i % per_seq, 0, 0))],
        out_shape=[jax.ShapeDtypeStruct((t, 2 * MIX_W), BF16),
                   jax.ShapeDtypeStruct((t, MEM_Q), F32),
                   jax.ShapeDtypeStruct((b, s // tk, MIX_W, tk), BF16)],
        compiler_params=_cparams("parallel"),
        name="sb_in_proj",
    )(x, g.reshape(1, D_MODEL), w_qk_qm, wvt)


def _split_rows(x):
    parts = []
    r = x
    for i in range(N_SPLIT):
        piece = r.astype(BF16)
        parts.append(piece)
        if i + 1 < N_SPLIT:
            r = r - piece.astype(F32)
    return jnp.concatenate(parts, axis=0)


def _sb_attn_kernel(q_ref, k_ref, vt_ref, u_ref, o_ref, *, tq, n_pairs):
    qi = pl.program_id(2)
    lane = lax.broadcasted_iota(jnp.int32, (tq, LANES), 1)
    key = lax.broadcasted_iota(jnp.int32, (tq, tq), 0)
    qry = lax.broadcasted_iota(jnp.int32, (tq, tq), 1)
    strict = key < qry
    u = u_ref[...]
    heads = [(p, h) for p in range(n_pairs) for h in range(2)]
    n = len(heads)
    qs = []
    for p, h in heads:
        in_head = (lane >= h * HEAD_DIM) & (lane < (h + 1) * HEAD_DIM)
        q_pair = q_ref[:, p * LANES:(p + 1) * LANES]
        qs.append(jnp.where(in_head, q_pair * (HEAD_DIM ** -0.5), 0).astype(BF16))

    def block(kb, state, diagonal):
        start = pl.multiple_of(kb * tq, tq)
        zs = []
        for idx in range(n):
            p = heads[idx][0]
            k = k_ref[pl.ds(start, tq), p * LANES:(p + 1) * LANES]
            zs.append(lax.dot_general(k, qs[idx], (((1,), (1,)), ((), ())),
                                      preferred_element_type=F32))
        log_betas, log_fails = [], []
        for z in zs:
            soft = jnp.log(1.0 + jnp.exp(-jnp.abs(z)))
            log_beta = jnp.minimum(z, 0.0) - soft
            log_fail = log_beta - z
            if diagonal:
                log_fail = jnp.where(strict, log_fail, 0.0)
            log_betas.append(log_beta)
            log_fails.append(log_fail)
        sums = [jnp.dot(u, _split_rows(lf), preferred_element_type=F32) for lf in log_fails]
        ws = []
        for idx in range(n):
            w = jnp.exp(log_betas[idx] + sums[idx][:tq] + state[idx][0])
            if diagonal:
                w = jnp.where(strict, w, 0.0)
            ws.append(w.astype(BF16))
        new = []
        for idx in range(n):
            vt = vt_ref[kb, idx * HEAD_DIM:(idx + 1) * HEAD_DIM, :]
            acc = state[idx][1] + jnp.dot(vt, ws[idx], preferred_element_type=F32)
            carry = state[idx][0] + sums[idx][tq:tq + 1]
            new.append((carry, acc))
        return tuple(new)

    def live(state):
        top = state[0][0]
        for carry, _ in state[1:]:
            top = jnp.maximum(top, carry)
        return jnp.max(top) >= EXP_UNDERFLOW

    zero = (jnp.zeros((1, tq), F32), jnp.zeros((HEAD_DIM, tq), F32))
    state = block(qi, (zero,) * n, True)

    def step(loop):
        i, _, state = loop
        state = block(qi - 1 - i, state, False)
        return i + 1, live(state), state

    _, _, state = lax.while_loop(lambda loop: (loop[0] < qi) & loop[1], step,
                                 (jnp.int32(0), live(state), state))
    for p in range(n_pairs):
        pair = jnp.concatenate([state[2 * p][1], state[2 * p + 1][1]], axis=0)
        o_ref[:, p * LANES:(p + 1) * LANES] = pair.T.astype(o_ref.dtype)


def _sb_attention(qk, vt, b, s, tq, n_pairs):
    r = jnp.arange(tq)
    u = (r[None, :] > r[:, None]).astype(BF16)
    u = jnp.concatenate([u, jnp.ones((U_EXTRA, tq), BF16)], axis=0)
    u = jnp.concatenate([u] * N_SPLIT, axis=1)
    kern = functools.partial(_sb_attn_kernel, tq=tq, n_pairs=n_pairs)
    w = n_pairs * LANES
    groups = N_PAIRS // n_pairs
    return pl.pallas_call(
        kern,
        grid=(b, groups, s // tq),
        in_specs=[
            pl.BlockSpec((None, tq, w), lambda bi, g, qi: (bi, qi, g)),
            pl.BlockSpec((None, s, w), lambda bi, g, qi: (bi, 0, groups + g)),
            pl.BlockSpec((None, s // tq, w, tq), lambda bi, g, qi: (bi, 0, g, 0)),
            pl.BlockSpec((tq + U_EXTRA, N_SPLIT * tq), lambda bi, g, qi: (0, 0)),
        ],
        out_specs=pl.BlockSpec((None, tq, w), lambda bi, g, qi: (bi, qi, g)),
        out_shape=jax.ShapeDtypeStruct((b, s, MIX_W), BF16),
        compiler_params=_cparams("parallel", "parallel", "arbitrary"),
        name="sb_attention",
    )(qk, qk, vt, u)


def _rope_table_kernel(pos_ref, invf_ref, sign_ref, cos_ref, sin_ref):
    ang = pos_ref[...].astype(F32) * invf_ref[...]
    cos_ref[...] = jnp.cos(ang)
    sin_ref[...] = jnp.sin(ang) * sign_ref[...]


def _rope_tables(positions, tm):
    t = positions.size
    half = MLA_ROPE // 2
    inv_freq = ROPE_THETA ** (-jnp.arange(half, dtype=F32) / half)
    zeros = jnp.zeros((MLA_NOPE,), F32)
    tail = jnp.zeros((LANES - MLA_QK,), F32)
    invf = jnp.concatenate([zeros, inv_freq, inv_freq, tail]).reshape(1, LANES)
    ones = jnp.ones((half,), F32)
    sign = jnp.concatenate([zeros, -ones, ones, tail]).reshape(1, LANES)
    return pl.pallas_call(
        _rope_table_kernel,
        grid=(t // tm,),
        in_specs=[
            pl.BlockSpec((tm, 1), lambda i: (i, 0)),
            pl.BlockSpec((1, LANES), lambda i: (0, 0)),
            pl.BlockSpec((1, LANES), lambda i: (0, 0)),
        ],
        out_specs=[pl.BlockSpec((tm, LANES), lambda i: (i, 0))] * 2,
        out_shape=[jax.ShapeDtypeStruct((t, LANES), F32)] * 2,
        compiler_params=_cparams("parallel"),
        name="rope_tables",
    )(positions.reshape(t, 1), invf, sign)


def _head_norm_rope(x, g, cosf, sinf):
    ms = jnp.sum(x * x, axis=-1, keepdims=True) / MLA_QK
    y = x * lax.rsqrt(ms + EPS) * g
    lane = lax.broadcasted_iota(jnp.int32, y.shape, 1)
    half = MLA_ROPE // 2
    partner = jnp.where(lane < MLA_NOPE + half,
                        pltpu.roll(y, LANES - half, 1), pltpu.roll(y, half, 1))
    return y * cosf + partner * sinf


def _mla_q_kernel(cq_ref, gq_ref, w_ref, gh_ref, cos_ref, sin_ref, q_ref):
    h = _rms_rows(cq_ref[...], gq_ref[...]).astype(BF16)
    cosf = cos_ref[...]
    sinf = sin_ref[...]
    g = gh_ref[...]
    for hd in range(N_HEADS):
        sl = slice(hd * LANES, (hd + 1) * LANES)
        x = jnp.dot(h, w_ref[:, sl], preferred_element_type=F32)
        q_ref[:, sl] = _head_norm_rope(x, g, cosf, sinf).astype(q_ref.dtype)


def _mla_k_kernel(ckv_ref, gkv_ref, w_ref, wvt_ref, kr_ref, gh_ref, cos_ref, sin_ref,
                  k_ref, vt_ref):
    h = _rms_rows(ckv_ref[...], gkv_ref[...]).astype(BF16)
    cosf = cos_ref[...]
    sinf = sin_ref[...]
    g = gh_ref[...]
    k_rope = kr_ref[...]
    for hd in range(N_HEADS):
        sl = slice(hd * LANES, (hd + 1) * LANES)
        x = jnp.dot(h, w_ref[:, sl], preferred_element_type=F32) + k_rope
        k_ref[:, sl] = _head_norm_rope(x, g, cosf, sinf).astype(k_ref.dtype)
    vt = lax.dot_general(wvt_ref[...], h, (((1,), (1,)), ((), ())), preferred_element_type=F32)
    tk = vt_ref.shape[-1]
    for c in range(vt_ref.shape[0]):
        vt_ref[c] = vt[:, c * tk:(c + 1) * tk].astype(vt_ref.dtype)


def _pad_head_gain(g):
    return jnp.concatenate([g.astype(F32), jnp.zeros((LANES - MLA_QK,), F32)]).reshape(1, LANES)


def _mla_prep(c_q, c_kv, k_rope, cosf, sinf, q_norm, w_q_up, kv_norm, w_kv_up, g_qn, g_kn,
              b, s, tm, tk):
    t = c_q.shape[0]
    pad = LANES - MLA_QK
    wq = jnp.pad(w_q_up.reshape(MLA_Q_RANK, N_HEADS, MLA_QK), ((0, 0), (0, 0), (0, pad)))
    wq = wq.reshape(MLA_Q_RANK, N_HEADS * LANES).astype(BF16)
    wkv = w_kv_up.reshape(MLA_KV_RANK, N_HEADS, MLA_NOPE + HEAD_DIM)
    wk = jnp.pad(wkv[:, :, :MLA_NOPE], ((0, 0), (0, 0), (0, LANES - MLA_NOPE)))
    wk = wk.reshape(MLA_KV_RANK, N_HEADS * LANES).astype(BF16)
    wvt = wkv[:, :, MLA_NOPE:].reshape(MLA_KV_RANK, MIX_W).T.astype(BF16)

    row = lambda i: (i, 0)
    fixed = lambda i: (0, 0)
    q = pl.pallas_call(
        _mla_q_kernel,
        grid=(t // tm,),
        in_specs=[
            pl.BlockSpec((tm, MLA_Q_RANK), row),
            pl.BlockSpec((1, MLA_Q_RANK), fixed),
            pl.BlockSpec((MLA_Q_RANK, N_HEADS * LANES), fixed),
            pl.BlockSpec((1, LANES), fixed),
            pl.BlockSpec((tm, LANES), row),
            pl.BlockSpec((tm, LANES), row),
        ],
        out_specs=pl.BlockSpec((tm, N_HEADS * LANES), row),
        out_shape=jax.ShapeDtypeStruct((t, N_HEADS * LANES), BF16),
        compiler_params=_cparams("parallel"),
        name="mla_q_prep",
    )(c_q, q_norm.reshape(1, -1), wq, _pad_head_gain(g_qn), cosf, sinf)
    per_seq = s // tm
    k, vt = pl.pallas_call(
        _mla_k_kernel,
        grid=(t // tm,),
        in_specs=[
            pl.BlockSpec((tm, MLA_KV_RANK), row),
            pl.BlockSpec((1, MLA_KV_RANK), fixed),
            pl.BlockSpec((MLA_KV_RANK, N_HEADS * LANES), fixed),
            pl.BlockSpec((MIX_W, MLA_KV_RANK), fixed),
            pl.BlockSpec((tm, LANES), row),
            pl.BlockSpec((1, LANES), fixed),
            pl.BlockSpec((tm, LANES), row),
            pl.BlockSpec((tm, LANES), row),
        ],
        out_specs=[pl.BlockSpec((tm, N_HEADS * LANES), row),
                   pl.BlockSpec((None, tm // tk, MIX_W, tk),
                                lambda i: (i // per_seq, i % per_seq, 0, 0))],
        out_shape=[jax.ShapeDtypeStruct((t, N_HEADS * LANES), BF16),
                   jax.ShapeDtypeStruct((b, s // tk, MIX_W, tk), BF16)],
        compiler_params=_cparams("parallel"),
        name="mla_kv_prep",
    )(c_kv, kv_norm.reshape(1, -1), wk, wvt, k_rope, _pad_head_gain(g_kn), cosf, sinf)
    return q, k, vt


def _causal_attn_kernel(q_ref, k_ref, vt_ref, o_ref, sc_ref, *, tq, n_pairs):
    qi = pl.program_id(2)
    key = lax.broadcasted_iota(jnp.int32, (tq, tq), 0)
    qry = lax.broadcasted_iota(jnp.int32, (tq, tq), 1)
    causal = key <= qry
    c = MLA_QK ** -0.5 * LOG2_E
    n_heads = 2 * n_pairs
    qs = [q_ref[:, h * LANES:(h + 1) * LANES] for h in range(n_heads)]

    def score_tiles(kb):
        start = pl.multiple_of(kb * tq, tq)
        return [lax.dot_general(k_ref[pl.ds(start, tq), h * LANES:(h + 1) * LANES], qs[h],
                                (((1,), (1,)), ((), ())), preferred_element_type=F32)
                for h in range(n_heads)]

    def park(scs, ms, diagonal):
        new_ms, alphas = [], []
        for h in range(n_heads):
            sc = jnp.where(causal, scs[h], -jnp.inf) if diagonal else scs[h]
            m_new = jnp.maximum(ms[h], jnp.max(sc, axis=0, keepdims=True))
            alphas.append(jnp.exp2((ms[h] - m_new) * c))
            new_ms.append(m_new)
            sc_ref[h] = sc
        return tuple(new_ms), tuple(alphas)

    def consume(kb, ms, alphas, ls, accs):
        new_ls, new_accs = [], []
        for h in range(n_heads):
            p = jnp.exp2((sc_ref[h] - ms[h]) * c)
            new_ls.append(alphas[h] * ls[h] + jnp.sum(p, axis=0, keepdims=True))
            vt = vt_ref[kb, h * HEAD_DIM:(h + 1) * HEAD_DIM, :]
            new_accs.append(alphas[h] * accs[h]
                            + jnp.dot(vt, p.astype(BF16), preferred_element_type=F32))
        return tuple(new_ls), tuple(new_accs)

    row = lambda v: (jnp.full((1, tq), v, F32),) * n_heads
    ms, alphas = park(score_tiles(qi), row(-jnp.inf), True)
    ls = row(0.0)
    accs = (jnp.zeros((HEAD_DIM, tq), F32),) * n_heads

    def trip(j, carry):
        ms, alphas, ls, accs = carry
        scs = score_tiles(j)
        ls, accs = consume(jnp.where(j == 0, qi, j - 1), ms, alphas, ls, accs)
        ms, alphas = park(scs, ms, False)
        return ms, alphas, ls, accs

    ms, alphas, ls, accs = lax.fori_loop(0, qi, trip, (ms, alphas, ls, accs))
    ls, accs = consume(jnp.where(qi == 0, qi, qi - 1), ms, alphas, ls, accs)
    outs = [accs[h] / ls[h] for h in range(n_heads)]
    for p in range(n_pairs):
        pair = jnp.concatenate([outs[2 * p], outs[2 * p + 1]], axis=0)
        o_ref[:, p * LANES:(p + 1) * LANES] = pair.T.astype(o_ref.dtype)


def _causal_attention(q, k, vt, b, s, tq, n_pairs):
    kern = functools.partial(_causal_attn_kernel, tq=tq, n_pairs=n_pairs)
    wq = 2 * n_pairs * LANES
    wv = n_pairs * LANES
    return pl.pallas_call(
        kern,
        grid=(b, N_PAIRS // n_pairs, s // tq),
        in_specs=[
            pl.BlockSpec((None, tq, wq), lambda bi, g, qi: (bi, qi, g)),
            pl.BlockSpec((None, s, wq), lambda bi, g, qi: (bi, 0, g)),
            pl.BlockSpec((None, s // tq, wv, tq), lambda bi, g, qi: (bi, 0, g, 0)),
        ],
        out_specs=pl.BlockSpec((None, tq, wv), lambda bi, g, qi: (bi, qi, g)),
        out_shape=jax.ShapeDtypeStruct((b, s, MIX_W), BF16),
        scratch_shapes=[pltpu.VMEM((2 * n_pairs, tq, tq), F32)],
        compiler_params=_cparams("parallel", "parallel", "arbitrary"),
        name="causal_attention",
    )(q, k, vt)


def _mem_kv_kernel(kv_ref, g_ref, k_ref, v_ref):
    k = kv_ref[:, :MEM_Q]
    v = kv_ref[:, MEM_Q:]
    g = g_ref[...]
    lane = lax.broadcasted_iota(jnp.int32, k.shape, 1)
    for h in range(N_MEM_HEADS):
        in_head = (lane >= h * HEAD_DIM) & (lane < (h + 1) * HEAD_DIM)
        ms = jnp.sum(jnp.where(in_head, k * k, 0.0), axis=-1, keepdims=True) / HEAD_DIM
        k_ref[h] = jnp.where(in_head, k * lax.rsqrt(ms + EPS) * g, 0.0).astype(k_ref.dtype)
        v_ref[h] = jnp.where(in_head, v, 0.0).astype(v_ref.dtype)


def _mem_out_kernel(mix_ref, qm_ref, k_ref, v_ref, g_ref, w_ref, x_ref, o_ref):
    q = qm_ref[...]
    lane = lax.broadcasted_iota(jnp.int32, q.shape, 1)
    inv = jnp.zeros_like(q)
    for h in range(N_MEM_HEADS):
        in_head = (lane >= h * HEAD_DIM) & (lane < (h + 1) * HEAD_DIM)
        ms = jnp.sum(jnp.where(in_head, q * q, 0.0), axis=-1, keepdims=True) / HEAD_DIM
        inv = jnp.where(in_head, lax.rsqrt(ms + EPS), inv)
    qn = (q * inv * g_ref[...]).astype(BF16)
    mem_o = jnp.zeros(q.shape, F32)
    for h in range(N_MEM_HEADS):
        sc = lax.dot_general(qn, k_ref[h], (((1,), (1,)), ((), ())), preferred_element_type=F32)
        sc = sc * (HEAD_DIM ** -0.5)
        e = jnp.exp(sc - jnp.max(sc, axis=-1, keepdims=True))
        p = e / jnp.sum(e, axis=-1, keepdims=True)
        mem_o = mem_o + jnp.dot(p.astype(BF16), v_ref[h], preferred_element_type=F32)
    out = jnp.dot(mix_ref[...], w_ref[:MIX_W, :], preferred_element_type=F32)
    out = out + jnp.dot(mem_o.astype(BF16), w_ref[MIX_W:, :], preferred_element_type=F32)
    o_ref[...] = x_ref[...] + out


def _mem_kv_prep(mem2d, ln_g, w_kv, g_kn, b, n_mem):
    w = w_kv.reshape(D_MODEL, N_MEM_HEADS, 2, HEAD_DIM).transpose(0, 2, 1, 3)
    w = w.reshape(D_MODEL, 2 * MEM_Q).astype(BF16)
    (kv,) = _rms_matmul(mem2d, ln_g, w, [(0, 2 * MEM_Q)], [F32], tm=n_mem, name="mem_kv_proj")
    shape = jax.ShapeDtypeStruct((b, N_MEM_HEADS, n_mem, MEM_Q), BF16)
    spec = pl.BlockSpec((None, N_MEM_HEADS, n_mem, MEM_Q), lambda i: (i, 0, 0, 0))
    return pl.pallas_call(
        _mem_kv_kernel,
        grid=(b,),
        in_specs=[pl.BlockSpec((n_mem, 2 * MEM_Q), lambda i: (i, 0)),
                  pl.BlockSpec((1, MEM_Q), lambda i: (0, 0))],
        out_specs=[spec, spec],
        out_shape=[shape, shape],
        compiler_params=_cparams("parallel"),
        name="mem_kv_prep",
    )(kv, jnp.tile(g_kn.astype(F32), N_MEM_HEADS).reshape(1, MEM_Q))


def _mem_out_proj(mix, q_mem, k_mem, v_mem, g_qn, w_out, x, s, tm):
    t = x.shape[0]
    n_mem = k_mem.shape[2]
    per_seq = s // tm
    row = lambda i: (i, 0)
    mem_spec = pl.BlockSpec((None, N_MEM_HEADS, n_mem, MEM_Q), lambda i: (i // per_seq, 0, 0, 0))
    return pl.pallas_call(
        _mem_out_kernel,
        grid=(t // tm,),
        in_specs=[
            pl.BlockSpec((tm, MIX_W), row),
            pl.BlockSpec((tm, MEM_Q), row),
            mem_spec,
            mem_spec,
            pl.BlockSpec((1, MEM_Q), lambda i: (0, 0)),
            pl.BlockSpec((D_MODEL, D_MODEL), lambda i: (0, 0)),
            pl.BlockSpec((tm, D_MODEL), row),
        ],
        out_specs=pl.BlockSpec((tm, D_MODEL), row),
        out_shape=jax.ShapeDtypeStruct((t, D_MODEL), F32),
        compiler_params=_cparams("parallel"),
        name="mem_attn_out_proj",
    )(mix, q_mem, k_mem, v_mem, jnp.tile(g_qn.astype(F32), N_MEM_HEADS).reshape(1, MEM_Q),
      w_out, x)


def _ffn_kernel(x_ref, g_ref, wg_ref, wu_ref, wd_ref, o_ref, h_ref, acc_ref):
    j = pl.program_id(1)

    @pl.when(j == 0)
    def _():
        h_ref[...] = _rms_rows(x_ref[...], g_ref[...]).astype(BF16)
        acc_ref[...] = x_ref[...]

    h = h_ref[...]
    gate = jnp.dot(h, wg_ref[...], preferred_element_type=F32)
    up = jnp.dot(h, wu_ref[...], preferred_element_type=F32)
    act = gate * (1.0 / (1.0 + jnp.exp(-gate))) * up
    acc_ref[...] += jnp.dot(act.astype(BF16), wd_ref[...], preferred_element_type=F32)

    @pl.when(j == pl.num_programs(1) - 1)
    def _():
        o_ref[...] = acc_ref[...]


def _ffn(x, g, w_gu, w_down, tm, tf):
    t = x.shape[0]
    n_f = D_FF // tf
    return pl.pallas_call(
        _ffn_kernel,
        grid=(t // tm, n_f),
        in_specs=[
            pl.BlockSpec((tm, D_MODEL), lambda i, j: (i, 0)),
            pl.BlockSpec((1, D_MODEL), lambda i, j: (0, 0)),
            pl.BlockSpec((D_MODEL, tf), lambda i, j: (0, j)),
            pl.BlockSpec((D_MODEL, tf), lambda i, j: (0, n_f + j)),
            pl.BlockSpec((tf, D_MODEL), lambda i, j: (j, 0)),
        ],
        out_specs=pl.BlockSpec((tm, D_MODEL), lambda i, j: (i, 0)),
        out_shape=jax.ShapeDtypeStruct((t, D_MODEL), F32),
        scratch_shapes=[pltpu.VMEM((tm, D_MODEL), BF16), pltpu.VMEM((tm, D_MODEL), F32)],
        compiler_params=_cparams("parallel", "arbitrary"),
        name="swiglu_ffn",
    )(x, g.reshape(1, D_MODEL), w_gu, w_gu, w_down)


def kernel(x, mem, positions, ln_attn, w_out, ln_mem, w_mem_kv, g_qn_mem, g_kn_mem, ln_ffn,
           w_ffn_gu, w_ffn_down, sb_w_in, mla_w_in, mla_q_norm, mla_w_q_up, mla_kv_norm,
           mla_w_kv_up, mla_g_qn, mla_g_kn):
    b, s, d = x.shape
    n_mem = mem.shape[1]
    t = b * s
    x2 = x.reshape(t, d)
    mem2 = mem.reshape(b * n_mem, d)

    qk, q_mem, vt = _sb_in_proj(x2, ln_attn[0], sb_w_in[0], b, s, tm=512, tk=256)
    mix = _sb_attention(qk.reshape(b, s, 2 * MIX_W), vt, b, s, tq=256, n_pairs=2).reshape(t, MIX_W)
    k_mem, v_mem = _mem_kv_prep(mem2, ln_mem[0], w_mem_kv[0], g_kn_mem[0], b, n_mem)
    x2 = _mem_out_proj(mix, q_mem, k_mem, v_mem, g_qn_mem[0], w_out[0].astype(BF16), x2, s, tm=512)
    x2 = _ffn(x2, ln_ffn[0], w_ffn_gu[0].astype(BF16), w_ffn_down[0].astype(BF16), tm=1024, tf=256)

    w_in = mla_w_in[0]
    o_kr = MLA_Q_RANK + MLA_KV_RANK
    o_qm = o_kr + MLA_ROPE
    w_kr = jnp.pad(w_in[:, o_kr:o_qm], ((0, 0), (MLA_NOPE, LANES - MLA_QK)))
    w_in = jnp.concatenate([w_in[:, :o_kr], w_in[:, o_qm:], w_kr], axis=1).astype(BF16)
    c_q, c_kv, q_mem, k_rope = _rms_matmul(
        x2, ln_attn[1], w_in,
        [(0, MLA_Q_RANK), (MLA_Q_RANK, MLA_KV_RANK), (o_kr, MEM_Q), (o_kr + MEM_Q, LANES)],
        [F32, F32, F32, F32], tm=512, name="mla_in_proj")
    cosf, sinf = _rope_tables(positions, tm=512)
    q, k, vt = _mla_prep(c_q, c_kv, k_rope, cosf, sinf, mla_q_norm[0], mla_w_q_up[0],
                         mla_kv_norm[0], mla_w_kv_up[0], mla_g_qn[0], mla_g_kn[0], b, s,
                         tm=512, tk=256)
    mix = _causal_attention(q.reshape(b, s, -1), k.reshape(b, s, -1), vt,
                            b, s, tq=256, n_pairs=2).reshape(t, MIX_W)
    k_mem, v_mem = _mem_kv_prep(mem2, ln_mem[1], w_mem_kv[1], g_kn_mem[1], b, n_mem)
    x2 = _mem_out_proj(mix, q_mem, k_mem, v_mem, g_qn_mem[1], w_out[1].astype(BF16), x2, s, tm=512)
    x2 = _ffn(x2, ln_ffn[1], w_ffn_gu[1].astype(BF16), w_ffn_down[1].astype(BF16), tm=1024, tf=256)
    return x2.reshape(b, s, d)
```

```python
import functools

import jax
import jax.numpy as jnp
from jax import lax
from jax.experimental import pallas as pl
from jax.experimental.pallas import tpu as pltpu

F32 = jnp.float32
BF16 = jnp.bfloat16

D_MODEL = 1024
HEAD_DIM = 64
N_HEADS = 12
N_PAIRS = N_HEADS // 2
MIX_W = N_HEADS * HEAD_DIM
MLA_Q_RANK = 768
MLA_KV_RANK = 256
MLA_NOPE = 64
MLA_ROPE = 32
MLA_QK = MLA_NOPE + MLA_ROPE
N_MEM_HEADS = 4
MEM_Q = N_MEM_HEADS * HEAD_DIM
D_FF = 2816
ROPE_THETA = 10000.0
EPS = 1e-6
LOG2_E = 1.4426950408889634
MEM_ROWS = 256
LANES = 128

VMEM_LIMIT = 56 * 1024 * 1024


def _cparams(*sem):
    return pltpu.CompilerParams(dimension_semantics=sem, vmem_limit_bytes=VMEM_LIMIT)


def _rms_rows(x, g):
    ms = jnp.mean(x * x, axis=-1, keepdims=True)
    return x * lax.rsqrt(ms + EPS) * g


def _rms_matmul_kernel(x_ref, g_ref, w_ref, *o_refs, segments, chunk):
    h = _rms_rows(x_ref[...].astype(F32), g_ref[...]).astype(BF16)
    for o_ref, (start, width) in zip(o_refs, segments):
        for c in range(0, width, chunk):
            cw = min(chunk, width - c)
            acc = jnp.dot(h, w_ref[:, start + c:start + c + cw], preferred_element_type=F32)
            o_ref[:, c:c + cw] = acc.astype(o_ref.dtype)


def _rms_matmul(x, g, w, segments, dtypes, tm, name):
    t, k = x.shape
    n = w.shape[1]
    kern = functools.partial(_rms_matmul_kernel, segments=tuple(segments), chunk=512)
    return pl.pallas_call(
        kern,
        grid=(t // tm,),
        in_specs=[
            pl.BlockSpec((tm, k), lambda i: (i, 0)),
            pl.BlockSpec((1, k), lambda i: (0, 0)),
            pl.BlockSpec((k, n), lambda i: (0, 0)),
        ],
        out_specs=[pl.BlockSpec((tm, wd), lambda i: (i, 0)) for _, wd in segments],
        out_shape=[jax.ShapeDtypeStruct((t, wd), dt) for (_, wd), dt in zip(segments, dtypes)],
        compiler_params=_cparams("parallel"),
        name=name,
    )(x, g.reshape(1, k), w)


N_SPLIT = 2
EXP_UNDERFLOW = -104.0
U_EXTRA = 16


def _sb_in_proj_kernel(x_ref, g_ref, w_ref, wvt_ref, qk_ref, qm_ref, vt_ref):
    h = _rms_rows(x_ref[...], g_ref[...]).astype(BF16)
    n_qk = qk_ref.shape[1]
    for c in range(0, n_qk, 512):
        qk_ref[:, c:c + 512] = jnp.dot(h, w_ref[:, c:c + 512],
                                       preferred_element_type=F32).astype(qk_ref.dtype)
    qm_ref[...] = jnp.dot(h, w_ref[:, n_qk:], preferred_element_type=F32)
    vt = lax.dot_general(wvt_ref[...], h, (((1,), (1,)), ((), ())), preferred_element_type=F32)
    tk = vt_ref.shape[-1]
    for c in range(vt_ref.shape[0]):
        vt_ref[c] = vt[:, c * tk:(c + 1) * tk].astype(vt_ref.dtype)


def _sb_in_proj(x, g, w_in, b, s, tm, tk):
    t = x.shape[0]
    w_qk_qm = jnp.concatenate([w_in[:, :2 * MIX_W], w_in[:, 3 * MIX_W:]], axis=1).astype(BF16)
    wvt = w_in[:, 2 * MIX_W:3 * MIX_W].T.astype(BF16)
    per_seq = s // tm
    row = lambda i: (i, 0)
    fixed = lambda i: (0, 0)
    return pl.pallas_call(
        _sb_in_proj_kernel,
        grid=(t // tm,),
        in_specs=[
            pl.BlockSpec((tm, D_MODEL), row),
            pl.BlockSpec((1, D_MODEL), fixed),
            pl.BlockSpec((D_MODEL, 2 * MIX_W + MEM_Q), fixed),
            pl.BlockSpec((MIX_W, D_MODEL), fixed),
        ],
        out_specs=[pl.BlockSpec((tm, 2 * MIX_W), row), pl.BlockSpec((tm, MEM_Q), row),
                   pl.BlockSpec((None, tm // tk, MIX_W, tk),
                                lambda i: (i // per_seq, i % per_seq, 0, 0))],
        out_shape=[jax.ShapeDtypeStruct((t, 2 * MIX_W), BF16),
                   jax.ShapeDtypeStruct((t, MEM_Q), F32),
                   jax.ShapeDtypeStruct((b, s // tk, MIX_W, tk), BF16)],
        compiler_params=_cparams("parallel"),
        name="sb_in_proj",
    )(x, g.reshape(1, D_MODEL), w_qk_qm, wvt)


def _split_rows(x):
    parts = []
    r = x
    for i in range(N_SPLIT):
        piece = r.astype(BF16)
        parts.append(piece)
        if i + 1 < N_SPLIT:
            r = r - piece.astype(F32)
    return jnp.concatenate(parts, axis=0)


def _sb_attn_kernel(q_ref, k_ref, vt_ref, u_ref, o_ref, *, tq, n_pairs):
    qi = pl.program_id(2)
    lane = lax.broadcasted_iota(jnp.int32, (tq, LANES), 1)
    key = lax.broadcasted_iota(jnp.int32, (tq, tq), 0)
    qry = lax.broadcasted_iota(jnp.int32, (tq, tq), 1)
    strict = key < qry
    u = u_ref[...]
    heads = [(p, h) for p in range(n_pairs) for h in range(2)]
    n = len(heads)
    qs = []
    for p, h in heads:
        in_head = (lane >= h * HEAD_DIM) & (lane < (h + 1) * HEAD_DIM)
        q_pair = q_ref[:, p * LANES:(p + 1) * LANES]
        qs.append(jnp.where(in_head, q_pair * (HEAD_DIM ** -0.5), 0).astype(BF16))

    def block(kb, state, diagonal):
        start = pl.multiple_of(kb * tq, tq)
        zs = []
        for idx in range(n):
            p = heads[idx][0]
            k = k_ref[pl.ds(start, tq), p * LANES:(p + 1) * LANES]
            zs.append(lax.dot_general(k, qs[idx], (((1,), (1,)), ((), ())),
                                      preferred_element_type=F32))
        log_betas, log_fails = [], []
        for z in zs:
            soft = jnp.log(1.0 + jnp.exp(-jnp.abs(z)))
            log_beta = jnp.minimum(z, 0.0) - soft
            log_fail = log_beta - z
            if diagonal:
                log_fail = jnp.where(strict, log_fail, 0.0)
            log_betas.append(log_beta)
            log_fails.append(log_fail)
        sums = [jnp.dot(u, _split_rows(lf), preferred_element_type=F32) for lf in log_fails]
        ws = []
        for idx in range(n):
            w = jnp.exp(log_betas[idx] + sums[idx][:tq] + state[idx][0])
            if diagonal:
                w = jnp.where(strict, w, 0.0)
            ws.append(w.astype(BF16))
        new = []
        for idx in range(n):
            vt = vt_ref[kb, idx * HEAD_DIM:(idx + 1) * HEAD_DIM, :]
            acc = state[idx][1] + jnp.dot(vt, ws[idx], preferred_element_type=F32)
            carry = state[idx][0] + sums[idx][tq:tq + 1]
            new.append((carry, acc))
        return tuple(new)

    def live(state):
        top = state[0][0]
        for carry, _ in state[1:]:
            top = jnp.maximum(top, carry)
        return jnp.max(top) >= EXP_UNDERFLOW

    zero = (jnp.zeros((1, tq), F32), jnp.zeros((HEAD_DIM, tq), F32))
    state = block(qi, (zero,) * n, True)

    def step(loop):
        i, _, state = loop
        state = block(qi - 1 - i, state, False)
        return i + 1, live(state), state

    _, _, state = lax.while_loop(lambda loop: (loop[0] < qi) & loop[1], step,
                                 (jnp.int32(0), live(state), state))
    for p in range(n_pairs):
        pair = jnp.concatenate([state[2 * p][1], state[2 * p + 1][1]], axis=0)
        o_ref[:, p * LANES:(p + 1) * LANES] = pair.T.astype(o_ref.dtype)


def _sb_attention(qk, vt, b, s, tq, n_pairs):
    r = jnp.arange(tq)
    u = (r[None, :] > r[:, None]).astype(BF16)
    u = jnp.concatenate([u, jnp.ones((U_EXTRA, tq), BF16)], axis=0)
    u = jnp.concatenate([u] * N_SPLIT, axis=1)
    kern = functools.partial(_sb_attn_kernel, tq=tq, n_pairs=n_pairs)
    w = n_pairs * LANES
    groups = N_PAIRS // n_pairs
    return pl.pallas_call(
        kern,
        grid=(b, groups, s // tq),
        in_specs=[
            pl.BlockSpec((None, tq, w), lambda bi, g, qi: (bi, qi, g)),
            pl.BlockSpec((None, s, w), lambda bi, g, qi: (bi, 0, groups + g)),
            pl.BlockSpec((None, s // tq, w, tq), lambda bi, g, qi: (bi, 0, g, 0)),
            pl.BlockSpec((tq + U_EXTRA, N_SPLIT * tq), lambda bi, g, qi: (0, 0)),
        ],
        out_specs=pl.BlockSpec((None, tq, w), lambda bi, g, qi: (bi, qi, g)),
        out_shape=jax.ShapeDtypeStruct((b, s, MIX_W), BF16),
        compiler_params=_cparams("parallel", "parallel", "arbitrary"),
        name="sb_attention",
    )(qk, qk, vt, u)


def _rope_table_kernel(pos_ref, invf_ref, sign_ref, keep_ref, cos_ref, sin_ref):
    ang = pos_ref[...].astype(F32) * invf_ref[...]
    cos_ref[...] = jnp.cos(ang) * keep_ref[...]
    sin_ref[...] = jnp.sin(ang) * sign_ref[...]


ROPE_HALF = MLA_ROPE // 2
ROT_LO = MLA_NOPE + ROPE_HALF
ROT_HI = ROT_LO + MLA_ROPE


def _rope_tables(positions, tm):
    t = positions.size
    inv_freq = ROPE_THETA ** (-jnp.arange(ROPE_HALF, dtype=F32) / ROPE_HALF)
    zeros = lambda n: jnp.zeros((n,), F32)
    ones = lambda n: jnp.ones((n,), F32)
    invf = jnp.concatenate([zeros(ROT_LO), inv_freq, inv_freq, zeros(LANES - ROT_HI)])
    sign = jnp.concatenate([zeros(ROT_LO), ones(ROPE_HALF), -ones(ROPE_HALF),
                            zeros(LANES - ROT_HI)])
    keep = jnp.concatenate([ones(MLA_NOPE), zeros(ROPE_HALF), ones(MLA_ROPE),
                            zeros(LANES - ROT_HI)])
    fixed = pl.BlockSpec((1, LANES), lambda i: (0, 0))
    return pl.pallas_call(
        _rope_table_kernel,
        grid=(t // tm,),
        in_specs=[pl.BlockSpec((tm, 1), lambda i: (i, 0)), fixed, fixed, fixed],
        out_specs=[pl.BlockSpec((tm, LANES), lambda i: (i, 0))] * 2,
        out_shape=[jax.ShapeDtypeStruct((t, LANES), F32)] * 2,
        compiler_params=_cparams("parallel"),
        name="rope_tables",
    )(positions.reshape(t, 1), invf.reshape(1, LANES), sign.reshape(1, LANES),
      keep.reshape(1, LANES))


def _head_norm_rope(x, g, real, cosf, sinf):
    ms = jnp.sum(x * x * real, axis=-1, keepdims=True) / MLA_QK
    y = x * lax.rsqrt(ms + EPS) * g
    return y * cosf + pltpu.roll(y, ROPE_HALF, 1) * sinf


def _mla_q_kernel(cq_ref, gq_ref, w_ref, gh_ref, real_ref, cos_ref, sin_ref, q_ref):
    h = _rms_rows(cq_ref[...], gq_ref[...]).astype(BF16)
    cosf = cos_ref[...]
    sinf = sin_ref[...]
    g = gh_ref[...]
    real = real_ref[...]
    for hd in range(N_HEADS):
        sl = slice(hd * LANES, (hd + 1) * LANES)
        x = jnp.dot(h, w_ref[:, sl], preferred_element_type=F32)
        q_ref[:, sl] = _head_norm_rope(x, g, real, cosf, sinf).astype(q_ref.dtype)


def _mla_k_kernel(ckv_ref, gkv_ref, w_ref, wvt_ref, kr_ref, gh_ref, real_ref, cos_ref, sin_ref,
                  k_ref, vt_ref):
    h = _rms_rows(ckv_ref[...], gkv_ref[...]).astype(BF16)
    cosf = cos_ref[...]
    sinf = sin_ref[...]
    g = gh_ref[...]
    real = real_ref[...]
    k_rope = kr_ref[...]
    for hd in range(N_HEADS):
        sl = slice(hd * LANES, (hd + 1) * LANES)
        x = jnp.dot(h, w_ref[:, sl], preferred_element_type=F32) + k_rope
        k_ref[:, sl] = _head_norm_rope(x, g, real, cosf, sinf).astype(k_ref.dtype)
    vt = lax.dot_general(wvt_ref[...], h, (((1,), (1,)), ((), ())), preferred_element_type=F32)
    tk = vt_ref.shape[-1]
    for c in range(vt_ref.shape[0]):
        vt_ref[c] = vt[:, c * tk:(c + 1) * tk].astype(vt_ref.dtype)


def _pad_head_gain(g):
    g = g.astype(F32)
    return jnp.concatenate([g, g[MLA_NOPE:]]).reshape(1, LANES)


def _mla_prep(c_q, c_kv, k_rope, cosf, sinf, q_norm, w_q_up, kv_norm, w_kv_up, g_qn, g_kn,
              b, s, tm, tk):
    t = c_q.shape[0]
    wq = w_q_up.reshape(MLA_Q_RANK, N_HEADS, MLA_QK)
    wq = jnp.concatenate([wq, wq[:, :, MLA_NOPE:]], axis=2)
    wq = wq.reshape(MLA_Q_RANK, N_HEADS * LANES).astype(BF16)
    real = (jnp.arange(LANES) < MLA_QK).astype(F32).reshape(1, LANES)
    wkv = w_kv_up.reshape(MLA_KV_RANK, N_HEADS, MLA_NOPE + HEAD_DIM)
    wk = jnp.pad(wkv[:, :, :MLA_NOPE], ((0, 0), (0, 0), (0, LANES - MLA_NOPE)))
    wk = wk.reshape(MLA_KV_RANK, N_HEADS * LANES).astype(BF16)
    wvt = wkv[:, :, MLA_NOPE:].reshape(MLA_KV_RANK, MIX_W).T.astype(BF16)

    row = lambda i: (i, 0)
    fixed = lambda i: (0, 0)
    q = pl.pallas_call(
        _mla_q_kernel,
        grid=(t // tm,),
        in_specs=[
            pl.BlockSpec((tm, MLA_Q_RANK), row),
            pl.BlockSpec((1, MLA_Q_RANK), fixed),
            pl.BlockSpec((MLA_Q_RANK, N_HEADS * LANES), fixed),
            pl.BlockSpec((1, LANES), fixed),
            pl.BlockSpec((1, LANES), fixed),
            pl.BlockSpec((tm, LANES), row),
            pl.BlockSpec((tm, LANES), row),
        ],
        out_specs=pl.BlockSpec((tm, N_HEADS * LANES), row),
        out_shape=jax.ShapeDtypeStruct((t, N_HEADS * LANES), BF16),
        compiler_params=_cparams("parallel"),
        name="mla_q_prep",
    )(c_q, q_norm.reshape(1, -1), wq, _pad_head_gain(g_qn), real, cosf, sinf)
    per_seq = s // tm
    k, vt = pl.pallas_call(
        _mla_k_kernel,
        grid=(t // tm,),
        in_specs=[
            pl.BlockSpec((tm, MLA_KV_RANK), row),
            pl.BlockSpec((1, MLA_KV_RANK), fixed),
            pl.BlockSpec((MLA_KV_RANK, N_HEADS * LANES), fixed),
            pl.BlockSpec((MIX_W, MLA_KV_RANK), fixed),
            pl.BlockSpec((tm, LANES), row),
            pl.BlockSpec((1, LANES), fixed),
            pl.BlockSpec((1, LANES), fixed),
            pl.BlockSpec((tm, LANES), row),
            pl.BlockSpec((tm, LANES), row),
        ],
        out_specs=[pl.BlockSpec((tm, N_HEADS * LANES), row),
                   pl.BlockSpec((None, tm // tk, MIX_W, tk),
                                lambda i: (i // per_seq, i % per_seq, 0, 0))],
        out_shape=[jax.ShapeDtypeStruct((t, N_HEADS * LANES), BF16),
                   jax.ShapeDtypeStruct((b, s // tk, MIX_W, tk), BF16)],
        compiler_params=_cparams("parallel"),
        name="mla_kv_prep",
    )(c_kv, kv_norm.reshape(1, -1), wk, wvt, k_rope, _pad_head_gain(g_kn), real, cosf, sinf)
    return q, k, vt


def _causal_attn_kernel(q_ref, k_ref, vt_ref, o_ref, sc_ref, *, tq, n_pairs):
    qi = pl.program_id(2)
    key = lax.broadcasted_iota(jnp.int32, (tq, tq), 0)
    qry = lax.broadcasted_iota(jnp.int32, (tq, tq), 1)
    causal = key <= qry
    c = MLA_QK ** -0.5 * LOG2_E
    n_heads = 2 * n_pairs
    qs = [q_ref[:, h * LANES:(h + 1) * LANES] for h in range(n_heads)]

    def score_tiles(kb):
        start = pl.multiple_of(kb * tq, tq)
        return [lax.dot_general(k_ref[pl.ds(start, tq), h * LANES:(h + 1) * LANES], qs[h],
                                (((1,), (1,)), ((), ())), preferred_element_type=F32)
                for h in range(n_heads)]

    def park(scs, ms, diagonal):
        new_ms, alphas = [], []
        for h in range(n_heads):
            sc = jnp.where(causal, scs[h], -jnp.inf) if diagonal else scs[h]
            m_new = jnp.maximum(ms[h], jnp.max(sc, axis=0, keepdims=True))
            alphas.append(jnp.exp2((ms[h] - m_new) * c))
            new_ms.append(m_new)
            sc_ref[h] = sc
        return tuple(new_ms), tuple(alphas)

    def consume(kb, ms, alphas, ls, accs):
        new_ls, new_accs = [], []
        for h in range(n_heads):
            p = jnp.exp2((sc_ref[h] - ms[h]) * c)
            new_ls.append(alphas[h] * ls[h] + jnp.sum(p, axis=0, keepdims=True))
            vt = vt_ref[kb, h * HEAD_DIM:(h + 1) * HEAD_DIM, :]
            new_accs.append(alphas[h] * accs[h]
                            + jnp.dot(vt, p.astype(BF16), preferred_element_type=F32))
        return tuple(new_ls), tuple(new_accs)

    row = lambda v: (jnp.full((1, tq), v, F32),) * n_heads
    ms, alphas = park(score_tiles(qi), row(-jnp.inf), True)
    ls = row(0.0)
    accs = (jnp.zeros((HEAD_DIM, tq), F32),) * n_heads

    def trip(j, carry):
        ms, alphas, ls, accs = carry
        scs = score_tiles(j)
        ls, accs = consume(jnp.where(j == 0, qi, j - 1), ms, alphas, ls, accs)
        ms, alphas = park(scs, ms, False)
        return ms, alphas, ls, accs

    ms, alphas, ls, accs = lax.fori_loop(0, qi, trip, (ms, alphas, ls, accs))
    ls, accs = consume(jnp.where(qi == 0, qi, qi - 1), ms, alphas, ls, accs)
    outs = [accs[h] / ls[h] for h in range(n_heads)]
    for p in range(n_pairs):
        pair = jnp.concatenate([outs[2 * p], outs[2 * p + 1]], axis=0)
        o_ref[:, p * LANES:(p + 1) * LANES] = pair.T.astype(o_ref.dtype)


def _causal_attention(q, k, vt, b, s, tq, n_pairs):
    kern = functools.partial(_causal_attn_kernel, tq=tq, n_pairs=n_pairs)
    wq = 2 * n_pairs * LANES
    wv = n_pairs * LANES
    return pl.pallas_call(
        kern,
        grid=(b, N_PAIRS // n_pairs, s // tq),
        in_specs=[
            pl.BlockSpec((None, tq, wq), lambda bi, g, qi: (bi, qi, g)),
            pl.BlockSpec((None, s, wq), lambda bi, g, qi: (bi, 0, g)),
            pl.BlockSpec((None, s // tq, wv, tq), lambda bi, g, qi: (bi, 0, g, 0)),
        ],
        out_specs=pl.BlockSpec((None, tq, wv), lambda bi, g, qi: (bi, qi, g)),
        out_shape=jax.ShapeDtypeStruct((b, s, MIX_W), BF16),
        scratch_shapes=[pltpu.VMEM((2 * n_pairs, tq, tq), F32)],
        compiler_params=_cparams("parallel", "parallel", "arbitrary"),
        name="causal_attention",
    )(q, k, vt)


def _mem_kv_kernel(kv_ref, g_ref, k_ref, v_ref):
    k = kv_ref[:, :MEM_Q]
    v = kv_ref[:, MEM_Q:]
    g = g_ref[...]
    lane = lax.broadcasted_iota(jnp.int32, k.shape, 1)
    for h in range(N_MEM_HEADS):
        in_head = (lane >= h * HEAD_DIM) & (lane < (h + 1) * HEAD_DIM)
        ms = jnp.sum(jnp.where(in_head, k * k, 0.0), axis=-1, keepdims=True) / HEAD_DIM
        k_ref[h] = jnp.where(in_head, k * lax.rsqrt(ms + EPS) * g, 0.0).astype(k_ref.dtype)
        v_ref[h] = jnp.where(in_head, v, 0.0).astype(v_ref.dtype)


def _mem_out_kernel(mix_ref, qm_ref, k_ref, v_ref, g_ref, w_ref, x_ref, o_ref):
    tm = qm_ref.shape[0]
    rows = [slice(r, r + MEM_ROWS) for r in range(0, tm, MEM_ROWS)]
    outs = [jnp.dot(mix_ref[r, :], w_ref[:MIX_W, :], preferred_element_type=F32) for r in rows]
    lane = lax.broadcasted_iota(jnp.int32, (MEM_ROWS, MEM_Q), 1)
    qns = []
    for r in rows:
        q = qm_ref[r, :]
        inv = jnp.zeros_like(q)
        for h in range(N_MEM_HEADS):
            in_head = (lane >= h * HEAD_DIM) & (lane < (h + 1) * HEAD_DIM)
            ms = jnp.sum(jnp.where(in_head, q * q, 0.0), axis=-1, keepdims=True) / HEAD_DIM
            inv = jnp.where(in_head, lax.rsqrt(ms + EPS), inv)
        qns.append((q * inv * g_ref[...]).astype(BF16))
    scs = [[lax.dot_general(qn, k_ref[h], (((1,), (1,)), ((), ())), preferred_element_type=F32)
            for h in range(N_MEM_HEADS)] for qn in qns]
    ps = []
    for chunk in scs:
        ps.append([])
        for sc in chunk:
            sc = sc * (HEAD_DIM ** -0.5)
            e = jnp.exp(sc - jnp.max(sc, axis=-1, keepdims=True))
            ps[-1].append((e / jnp.sum(e, axis=-1, keepdims=True)).astype(BF16))
    for r, out, p in zip(rows, outs, ps):
        mem_o = jnp.dot(p[0], v_ref[0], preferred_element_type=F32)
        for h in range(1, N_MEM_HEADS):
            mem_o = mem_o + jnp.dot(p[h], v_ref[h], preferred_element_type=F32)
        out = out + jnp.dot(mem_o.astype(BF16), w_ref[MIX_W:, :], preferred_element_type=F32)
        o_ref[r, :] = x_ref[r, :] + out


def _mem_kv_prep(mem2d, ln_g, w_kv, g_kn, b, n_mem):
    w = w_kv.reshape(D_MODEL, N_MEM_HEADS, 2, HEAD_DIM).transpose(0, 2, 1, 3)
    w = w.reshape(D_MODEL, 2 * MEM_Q).astype(BF16)
    (kv,) = _rms_matmul(mem2d, ln_g, w, [(0, 2 * MEM_Q)], [F32], tm=n_mem, name="mem_kv_proj")
    shape = jax.ShapeDtypeStruct((b, N_MEM_HEADS, n_mem, MEM_Q), BF16)
    spec = pl.BlockSpec((None, N_MEM_HEADS, n_mem, MEM_Q), lambda i: (i, 0, 0, 0))
    return pl.pallas_call(
        _mem_kv_kernel,
        grid=(b,),
        in_specs=[pl.BlockSpec((n_mem, 2 * MEM_Q), lambda i: (i, 0)),
                  pl.BlockSpec((1, MEM_Q), lambda i: (0, 0))],
        out_specs=[spec, spec],
        out_shape=[shape, shape],
        compiler_params=_cparams("parallel"),
        name="mem_kv_prep",
    )(kv, jnp.tile(g_kn.astype(F32), N_MEM_HEADS).reshape(1, MEM_Q))


def _mem_out_proj(mix, q_mem, k_mem, v_mem, g_qn, w_out, x, s, tm):
    t = x.shape[0]
    n_mem = k_mem.shape[2]
    per_seq = s // tm
    row = lambda i: (i, 0)
    mem_spec = pl.BlockSpec((None, N_MEM_HEADS, n_mem, MEM_Q), lambda i: (i // per_seq, 0, 0, 0))
    return pl.pallas_call(
        _mem_out_kernel,
        grid=(t // tm,),
        in_specs=[
            pl.BlockSpec((tm, MIX_W), row),
            pl.BlockSpec((tm, MEM_Q), row),
            mem_spec,
            mem_spec,
            pl.BlockSpec((1, MEM_Q), lambda i: (0, 0)),
            pl.BlockSpec((D_MODEL, D_MODEL), lambda i: (0, 0)),
            pl.BlockSpec((tm, D_MODEL), row),
        ],
        out_specs=pl.BlockSpec((tm, D_MODEL), row),
        out_shape=jax.ShapeDtypeStruct((t, D_MODEL), F32),
        compiler_params=_cparams("parallel"),
        name="mem_attn_out_proj",
    )(mix, q_mem, k_mem, v_mem, jnp.tile(g_qn.astype(F32), N_MEM_HEADS).reshape(1, MEM_Q),
      w_out, x)


def _ffn_kernel(x_ref, g_ref, wgu_ref, wd_ref, o_ref, *, tf):
    x = x_ref[...]
    h = _rms_rows(x, g_ref[...]).astype(BF16)
    acc = x
    for c in range(0, D_FF, tf):
        gate = jnp.dot(h, wgu_ref[:, c:c + tf], preferred_element_type=F32)
        up = jnp.dot(h, wgu_ref[:, D_FF + c:D_FF + c + tf], preferred_element_type=F32)
        act = gate * (1.0 / (1.0 + jnp.exp(-gate))) * up
        acc = acc + jnp.dot(act.astype(BF16), wd_ref[c:c + tf, :], preferred_element_type=F32)
    o_ref[...] = acc


def _ffn(x, g, w_gu, w_down, tm, tf):
    t = x.shape[0]
    once = pl.Buffered(1)
    return pl.pallas_call(
        functools.partial(_ffn_kernel, tf=tf),
        grid=(t // tm,),
        in_specs=[
            pl.BlockSpec((tm, D_MODEL), lambda i: (i, 0)),
            pl.BlockSpec((1, D_MODEL), lambda i: (0, 0)),
            pl.BlockSpec((D_MODEL, 2 * D_FF), lambda i: (0, 0), pipeline_mode=once),
            pl.BlockSpec((D_FF, D_MODEL), lambda i: (0, 0), pipeline_mode=once),
        ],
        out_specs=pl.BlockSpec((tm, D_MODEL), lambda i: (i, 0)),
        out_shape=jax.ShapeDtypeStruct((t, D_MODEL), F32),
        compiler_params=_cparams("parallel"),
        name="swiglu_ffn",
    )(x, g.reshape(1, D_MODEL), w_gu, w_down)


def kernel(x, mem, positions, ln_attn, w_out, ln_mem, w_mem_kv, g_qn_mem, g_kn_mem, ln_ffn,
           w_ffn_gu, w_ffn_down, sb_w_in, mla_w_in, mla_q_norm, mla_w_q_up, mla_kv_norm,
           mla_w_kv_up, mla_g_qn, mla_g_kn):
    b, s, d = x.shape
    n_mem = mem.shape[1]
    t = b * s
    x2 = x.reshape(t, d)
    mem2 = mem.reshape(b * n_mem, d)

    qk, q_mem, vt = _sb_in_proj(x2, ln_attn[0], sb_w_in[0], b, s, tm=512, tk=256)
    mix = _sb_attention(qk.reshape(b, s, 2 * MIX_W), vt, b, s, tq=256, n_pairs=2).reshape(t, MIX_W)
    k_mem, v_mem = _mem_kv_prep(mem2, ln_mem[0], w_mem_kv[0], g_kn_mem[0], b, n_mem)
    x2 = _mem_out_proj(mix, q_mem, k_mem, v_mem, g_qn_mem[0], w_out[0].astype(BF16), x2, s, tm=1024)
    x2 = _ffn(x2, ln_ffn[0], w_ffn_gu[0].astype(BF16), w_ffn_down[0].astype(BF16), tm=1024, tf=256)

    w_in = mla_w_in[0]
    o_kr = MLA_Q_RANK + MLA_KV_RANK
    o_qm = o_kr + MLA_ROPE
    w_kr = jnp.concatenate([jnp.zeros((D_MODEL, MLA_NOPE), F32), w_in[:, o_kr:o_qm],
                            w_in[:, o_kr:o_qm]], axis=1)
    w_in = jnp.concatenate([w_in[:, :o_kr], w_in[:, o_qm:], w_kr], axis=1).astype(BF16)
    c_q, c_kv, q_mem, k_rope = _rms_matmul(
        x2, ln_attn[1], w_in,
        [(0, MLA_Q_RANK), (MLA_Q_RANK, MLA_KV_RANK), (o_kr, MEM_Q), (o_kr + MEM_Q, LANES)],
        [F32, F32, F32, F32], tm=512, name="mla_in_proj")
    cosf, sinf = _rope_tables(positions, tm=512)
    q, k, vt = _mla_prep(c_q, c_kv, k_rope, cosf, sinf, mla_q_norm[0], mla_w_q_up[0],
                         mla_kv_norm[0], mla_w_kv_up[0], mla_g_qn[0], mla_g_kn[0], b, s,
                         tm=512, tk=256)
    mix = _causal_attention(q.reshape(b, s, -1), k.reshape(b, s, -1), vt,
                            b, s, tq=256, n_pairs=2).reshape(t, MIX_W)
    k_mem, v_mem = _mem_kv_prep(mem2, ln_mem[1], w_mem_kv[1], g_kn_mem[1], b, n_mem)
    x2 = _mem_out_proj(mix, q_mem, k_mem, v_mem, g_qn_mem[1], w_out[1].astype(BF16), x2, s, tm=1024)
    x2 = _ffn(x2, ln_ffn[1], w_ffn_gu[1].astype(BF16), w_ffn_down[1].astype(BF16), tm=1024, tf=256)
    return x2.reshape(b, s, d)
```

```python
import functools

import jax
import jax.numpy as jnp
from jax import lax
from jax.experimental import pallas as pl
from jax.experimental.pallas import tpu as pltpu

F32 = jnp.float32
BF16 = jnp.bfloat16

D_MODEL = 1024
HEAD_DIM = 64
N_HEADS = 12
N_PAIRS = N_HEADS // 2
MIX_W = N_HEADS * HEAD_DIM
MLA_Q_RANK = 768
MLA_KV_RANK = 256
MLA_NOPE = 64
MLA_ROPE = 32
MLA_QK = MLA_NOPE + MLA_ROPE
N_MEM_HEADS = 4
MEM_Q = N_MEM_HEADS * HEAD_DIM
D_FF = 2816
ROPE_THETA = 10000.0
EPS = 1e-6
LOG2_E = 1.4426950408889634
MEM_ROWS = 256
LANES = 128

VMEM_LIMIT = 56 * 1024 * 1024


def _cparams(*sem):
    return pltpu.CompilerParams(dimension_semantics=sem, vmem_limit_bytes=VMEM_LIMIT)


def _rms_rows(x, g):
    ms = jnp.mean(x * x, axis=-1, keepdims=True)
    return x * lax.rsqrt(ms + EPS) * g


def _rms_matmul_kernel(x_ref, g_ref, w_ref, *o_refs, segments, chunk):
    h = _rms_rows(x_ref[...].astype(F32), g_ref[...]).astype(BF16)
    for o_ref, (start, width) in zip(o_refs, segments):
        for c in range(0, width, chunk):
            cw = min(chunk, width - c)
            acc = jnp.dot(h, w_ref[:, start + c:start + c + cw], preferred_element_type=F32)
            o_ref[:, c:c + cw] = acc.astype(o_ref.dtype)


def _rms_matmul(x, g, w, segments, dtypes, tm, name):
    t, k = x.shape
    n = w.shape[1]
    kern = functools.partial(_rms_matmul_kernel, segments=tuple(segments), chunk=512)
    return pl.pallas_call(
        kern,
        grid=(t // tm,),
        in_specs=[
            pl.BlockSpec((tm, k), lambda i: (i, 0)),
            pl.BlockSpec((1, k), lambda i: (0, 0)),
            pl.BlockSpec((k, n), lambda i: (0, 0)),
        ],
        out_specs=[pl.BlockSpec((tm, wd), lambda i: (i, 0)) for _, wd in segments],
        out_shape=[jax.ShapeDtypeStruct((t, wd), dt) for (_, wd), dt in zip(segments, dtypes)],
        compiler_params=_cparams("parallel"),
        name=name,
    )(x, g.reshape(1, k), w)


N_SPLIT = 2
EXP_UNDERFLOW = -104.0
U_EXTRA = 16


def _sb_in_proj_kernel(x_ref, g_ref, w_ref, wvt_ref, qk_ref, qm_ref, vt_ref):
    h = _rms_rows(x_ref[...], g_ref[...]).astype(BF16)
    n_qk = qk_ref.shape[1]
    for c in range(0, n_qk, 512):
        qk_ref[:, c:c + 512] = jnp.dot(h, w_ref[:, c:c + 512],
                                       preferred_element_type=F32).astype(qk_ref.dtype)
    qm_ref[...] = jnp.dot(h, w_ref[:, n_qk:], preferred_element_type=F32)
    vt = lax.dot_general(wvt_ref[...], h, (((1,), (1,)), ((), ())), preferred_element_type=F32)
    tk = vt_ref.shape[-1]
    for c in range(vt_ref.shape[0]):
        vt_ref[c] = vt[:, c * tk:(c + 1) * tk].astype(vt_ref.dtype)


def _sb_in_proj(x, g, w_in, b, s, tm, tk):
    t = x.shape[0]
    w_qk_qm = jnp.concatenate([w_in[:, :2 * MIX_W], w_in[:, 3 * MIX_W:]], axis=1).astype(BF16)
    wvt = w_in[:, 2 * MIX_W:3 * MIX_W].T.astype(BF16)
    per_seq = s // tm
    row = lambda i: (i, 0)
    fixed = lambda i: (0, 0)
    return pl.pallas_call(
        _sb_in_proj_kernel,
        grid=(t // tm,),
        in_specs=[
            pl.BlockSpec((tm, D_MODEL), row),
            pl.BlockSpec((1, D_MODEL), fixed),
            pl.BlockSpec((D_MODEL, 2 * MIX_W + MEM_Q), fixed),
            pl.BlockSpec((MIX_W, D_MODEL), fixed),
        ],
        out_specs=[pl.BlockSpec((tm, 2 * MIX_W), row), pl.BlockSpec((tm, MEM_Q), row),
                   pl.BlockSpec((None, tm // tk, MIX_W, tk),
                                lambda i: (i // per_seq, i % per_seq, 0, 0))],
        out_shape=[jax.ShapeDtypeStruct((t, 2 * MIX_W), BF16),
                   jax.ShapeDtypeStruct((t, MEM_Q), F32),
                   jax.ShapeDtypeStruct((b, s // tk, MIX_W, tk), BF16)],
        compiler_params=_cparams("parallel"),
        name="sb_in_proj",
    )(x, g.reshape(1, D_MODEL), w_qk_qm, wvt)


def _split_rows(x):
    parts = []
    r = x
    for i in range(N_SPLIT):
        piece = r.astype(BF16)
        parts.append(piece)
        if i + 1 < N_SPLIT:
            r = r - piece.astype(F32)
    return jnp.concatenate(parts, axis=0)


def _sb_attn_kernel(q_ref, k_ref, vt_ref, u_ref, o_ref, *, tq, n_pairs):
    qi = pl.program_id(2)
    lane = lax.broadcasted_iota(jnp.int32, (tq, LANES), 1)
    key = lax.broadcasted_iota(jnp.int32, (tq, tq), 0)
    qry = lax.broadcasted_iota(jnp.int32, (tq, tq), 1)
    strict = key < qry
    u = u_ref[...]
    heads = [(p, h) for p in range(n_pairs) for h in range(2)]
    n = len(heads)
    qs = []
    for p, h in heads:
        in_head = (lane >= h * HEAD_DIM) & (lane < (h + 1) * HEAD_DIM)
        q_pair = q_ref[:, p * LANES:(p + 1) * LANES]
        qs.append(jnp.where(in_head, q_pair * (HEAD_DIM ** -0.5), 0).astype(BF16))

    def score_stage(kb, diagonal):
        start = pl.multiple_of(kb * tq, tq)
        zs = []
        for idx in range(n):
            p = heads[idx][0]
            k = k_ref[pl.ds(start, tq), p * LANES:(p + 1) * LANES]
            zs.append(lax.dot_general(k, qs[idx], (((1,), (1,)), ((), ())),
                                      preferred_element_type=F32))
        log_betas, log_fails = [], []
        for z in zs:
            soft = jnp.log(1.0 + jnp.exp(-jnp.abs(z)))
            log_beta = jnp.minimum(z, 0.0) - soft
            log_fail = log_beta - z
            if diagonal:
                log_fail = jnp.where(strict, log_fail, 0.0)
            log_betas.append(log_beta)
            log_fails.append(log_fail)
        return log_betas, log_fails

    def sum_stage(log_fails):
        return [jnp.dot(u, _split_rows(lf), preferred_element_type=F32) for lf in log_fails]

    def value_stage(kb, log_betas, sums, state, diagonal):
        ws = []
        for idx in range(n):
            w = jnp.exp(log_betas[idx] + sums[idx][:tq] + state[idx][0])
            if diagonal:
                w = jnp.where(strict, w, 0.0)
            ws.append(w.astype(BF16))
        new = []
        for idx in range(n):
            vt = vt_ref[kb, idx * HEAD_DIM:(idx + 1) * HEAD_DIM, :]
            acc = state[idx][1] + jnp.dot(vt, ws[idx], preferred_element_type=F32)
            carry = state[idx][0] + sums[idx][tq:tq + 1]
            new.append((carry, acc))
        return tuple(new)

    def block(kb, state):
        log_betas, log_fails = score_stage(kb, False)
        return value_stage(kb, log_betas, sum_stage(log_fails), state, False)

    zero = ((jnp.zeros((1, tq), F32), jnp.zeros((HEAD_DIM, tq), F32)),) * n

    def diagonal_only():
        log_betas, log_fails = score_stage(qi, True)
        return value_stage(qi, log_betas, sum_stage(log_fails), zero, True)

    def diagonal_and_previous():
        lb_d, lf_d = score_stage(qi, True)
        lb_p, lf_p = score_stage(qi - 1, False)
        sums_d = sum_stage(lf_d)
        sums_p = sum_stage(lf_p)
        state = value_stage(qi, lb_d, sums_d, zero, True)
        return value_stage(qi - 1, lb_p, sums_p, state, False)

    def live(state):
        top = state[0][0]
        for carry, _ in state[1:]:
            top = jnp.maximum(top, carry)
        return jnp.max(top) >= EXP_UNDERFLOW

    state = lax.cond(qi > 0, diagonal_and_previous, diagonal_only)

    def step(loop):
        i, _, state = loop
        state = block(qi - 1 - i, state)
        return i + 1, live(state), state

    _, _, state = lax.while_loop(lambda loop: (loop[0] < qi) & loop[1], step,
                                 (jnp.int32(1), live(state), state))
    for p in range(n_pairs):
        pair = jnp.concatenate([state[2 * p][1], state[2 * p + 1][1]], axis=0)
        o_ref[:, p * LANES:(p + 1) * LANES] = pair.T.astype(o_ref.dtype)


def _sb_attention(qk, vt, b, s, tq, n_pairs):
    r = jnp.arange(tq)
    u = (r[None, :] > r[:, None]).astype(BF16)
    u = jnp.concatenate([u, jnp.ones((U_EXTRA, tq), BF16)], axis=0)
    u = jnp.concatenate([u] * N_SPLIT, axis=1)
    kern = functools.partial(_sb_attn_kernel, tq=tq, n_pairs=n_pairs)
    w = n_pairs * LANES
    groups = N_PAIRS // n_pairs
    return pl.pallas_call(
        kern,
        grid=(b, groups, s // tq),
        in_specs=[
            pl.BlockSpec((None, tq, w), lambda bi, g, qi: (bi, qi, g)),
            pl.BlockSpec((None, s, w), lambda bi, g, qi: (bi, 0, groups + g)),
            pl.BlockSpec((None, s // tq, w, tq), lambda bi, g, qi: (bi, 0, g, 0)),
            pl.BlockSpec((tq + U_EXTRA, N_SPLIT * tq), lambda bi, g, qi: (0, 0)),
        ],
        out_specs=pl.BlockSpec((None, tq, w), lambda bi, g, qi: (bi, qi, g)),
        out_shape=jax.ShapeDtypeStruct((b, s, MIX_W), BF16),
        compiler_params=_cparams("parallel", "parallel", "arbitrary"),
        name="sb_attention",
    )(qk, qk, vt, u)


def _rope_table_kernel(pos_ref, invf_ref, sign_ref, keep_ref, cos_ref, sin_ref):
    ang = pos_ref[...].astype(F32) * invf_ref[...]
    cos_ref[...] = jnp.cos(ang) * keep_ref[...]
    sin_ref[...] = jnp.sin(ang) * sign_ref[...]


ROPE_HALF = MLA_ROPE // 2
ROT_LO = MLA_NOPE + ROPE_HALF
ROT_HI = ROT_LO + MLA_ROPE


def _rope_tables(positions, tm):
    t = positions.size
    inv_freq = ROPE_THETA ** (-jnp.arange(ROPE_HALF, dtype=F32) / ROPE_HALF)
    zeros = lambda n: jnp.zeros((n,), F32)
    ones = lambda n: jnp.ones((n,), F32)
    invf = jnp.concatenate([zeros(ROT_LO), inv_freq, inv_freq, zeros(LANES - ROT_HI)])
    sign = jnp.concatenate([zeros(ROT_LO), ones(ROPE_HALF), -ones(ROPE_HALF),
                            zeros(LANES - ROT_HI)])
    keep = jnp.concatenate([ones(MLA_NOPE), zeros(ROPE_HALF), ones(MLA_ROPE),
                            zeros(LANES - ROT_HI)])
    fixed = pl.BlockSpec((1, LANES), lambda i: (0, 0))
    return pl.pallas_call(
        _rope_table_kernel,
        grid=(t // tm,),
        in_specs=[pl.BlockSpec((tm, 1), lambda i: (i, 0)), fixed, fixed, fixed],
        out_specs=[pl.BlockSpec((tm, LANES), lambda i: (i, 0))] * 2,
        out_shape=[jax.ShapeDtypeStruct((t, LANES), F32)] * 2,
        compiler_params=_cparams("parallel"),
        name="rope_tables",
    )(positions.reshape(t, 1), invf.reshape(1, LANES), sign.reshape(1, LANES),
      keep.reshape(1, LANES))


def _head_norm_rope(x, g, real, cosf, sinf):
    ms = jnp.sum(x * x * real, axis=-1, keepdims=True) / MLA_QK
    y = x * lax.rsqrt(ms + EPS) * g
    return y * cosf + pltpu.roll(y, ROPE_HALF, 1) * sinf


def _mla_q_kernel(cq_ref, gq_ref, w_ref, gh_ref, real_ref, cos_ref, sin_ref, q_ref):
    h = _rms_rows(cq_ref[...], gq_ref[...]).astype(BF16)
    cosf = cos_ref[...]
    sinf = sin_ref[...]
    g = gh_ref[...]
    real = real_ref[...]
    for hd in range(N_HEADS):
        sl = slice(hd * LANES, (hd + 1) * LANES)
        x = jnp.dot(h, w_ref[:, sl], preferred_element_type=F32)
        q_ref[:, sl] = _head_norm_rope(x, g, real, cosf, sinf).astype(q_ref.dtype)


def _mla_k_kernel(ckv_ref, gkv_ref, w_ref, wvt_ref, kr_ref, gh_ref, real_ref, cos_ref, sin_ref,
                  k_ref, vt_ref):
    h = _rms_rows(ckv_ref[...], gkv_ref[...]).astype(BF16)
    cosf = cos_ref[...]
    sinf = sin_ref[...]
    g = gh_ref[...]
    real = real_ref[...]
    k_rope = kr_ref[...]
    for hd in range(N_HEADS):
        sl = slice(hd * LANES, (hd + 1) * LANES)
        x = jnp.dot(h, w_ref[:, sl], preferred_element_type=F32) + k_rope
        k_ref[:, sl] = _head_norm_rope(x, g, real, cosf, sinf).astype(k_ref.dtype)
    vt = lax.dot_general(wvt_ref[...], h, (((1,), (1,)), ((), ())), preferred_element_type=F32)
    tk = vt_ref.shape[-1]
    for c in range(vt_ref.shape[0]):
        vt_ref[c] = vt[:, c * tk:(c + 1) * tk].astype(vt_ref.dtype)


def _pad_head_gain(g):
    g = g.astype(F32)
    return jnp.concatenate([g, g[MLA_NOPE:]]).reshape(1, LANES)


def _mla_prep(c_q, c_kv, k_rope, cosf, sinf, q_norm, w_q_up, kv_norm, w_kv_up, g_qn, g_kn,
              b, s, tm, tk):
    t = c_q.shape[0]
    wq = w_q_up.reshape(MLA_Q_RANK, N_HEADS, MLA_QK)
    wq = jnp.concatenate([wq, wq[:, :, MLA_NOPE:]], axis=2)
    wq = wq.reshape(MLA_Q_RANK, N_HEADS * LANES).astype(BF16)
    real = (jnp.arange(LANES) < MLA_QK).astype(F32).reshape(1, LANES)
    wkv = w_kv_up.reshape(MLA_KV_RANK, N_HEADS, MLA_NOPE + HEAD_DIM)
    wk = jnp.pad(wkv[:, :, :MLA_NOPE], ((0, 0), (0, 0), (0, LANES - MLA_NOPE)))
    wk = wk.reshape(MLA_KV_RANK, N_HEADS * LANES).astype(BF16)
    wvt = wkv[:, :, MLA_NOPE:].reshape(MLA_KV_RANK, MIX_W).T.astype(BF16)

    row = lambda i: (i, 0)
    fixed = lambda i: (0, 0)
    q = pl.pallas_call(
        _mla_q_kernel,
        grid=(t // tm,),
        in_specs=[
            pl.BlockSpec((tm, MLA_Q_RANK), row),
            pl.BlockSpec((1, MLA_Q_RANK), fixed),
            pl.BlockSpec((MLA_Q_RANK, N_HEADS * LANES), fixed),
            pl.BlockSpec((1, LANES), fixed),
            pl.BlockSpec((1, LANES), fixed),
            pl.BlockSpec((tm, LANES), row),
            pl.BlockSpec((tm, LANES), row),
        ],
        out_specs=pl.BlockSpec((tm, N_HEADS * LANES), row),
        out_shape=jax.ShapeDtypeStruct((t, N_HEADS * LANES), BF16),
        compiler_params=_cparams("parallel"),
        name="mla_q_prep",
    )(c_q, q_norm.reshape(1, -1), wq, _pad_head_gain(g_qn), real, cosf, sinf)
    per_seq = s // tm
    k, vt = pl.pallas_call(
        _mla_k_kernel,
        grid=(t // tm,),
        in_specs=[
            pl.BlockSpec((tm, MLA_KV_RANK), row),
            pl.BlockSpec((1, MLA_KV_RANK), fixed),
            pl.BlockSpec((MLA_KV_RANK, N_HEADS * LANES), fixed),
            pl.BlockSpec((MIX_W, MLA_KV_RANK), fixed),
            pl.BlockSpec((tm, LANES), row),
            pl.BlockSpec((1, LANES), fixed),
            pl.BlockSpec((1, LANES), fixed),
            pl.BlockSpec((tm, LANES), row),
            pl.BlockSpec((tm, LANES), row),
        ],
        out_specs=[pl.BlockSpec((tm, N_HEADS * LANES), row),
                   pl.BlockSpec((None, tm // tk, MIX_W, tk),
                                lambda i: (i // per_seq, i % per_seq, 0, 0))],
        out_shape=[jax.ShapeDtypeStruct((t, N_HEADS * LANES), BF16),
                   jax.ShapeDtypeStruct((b, s // tk, MIX_W, tk), BF16)],
        compiler_params=_cparams("parallel"),
        name="mla_kv_prep",
    )(c_kv, kv_norm.reshape(1, -1), wk, wvt, k_rope, _pad_head_gain(g_kn), real, cosf, sinf)
    return q, k, vt


def _causal_attn_kernel(q_ref, k_ref, vt_ref, o_ref, sc_ref, *, tq, n_pairs):
    qi = pl.program_id(2)
    key = lax.broadcasted_iota(jnp.int32, (tq, tq), 0)
    qry = lax.broadcasted_iota(jnp.int32, (tq, tq), 1)
    causal = key <= qry
    c = MLA_QK ** -0.5 * LOG2_E
    n_heads = 2 * n_pairs
    qs = [q_ref[:, h * LANES:(h + 1) * LANES] for h in range(n_heads)]

    def score_tiles(kb):
        start = pl.multiple_of(kb * tq, tq)
        return [lax.dot_general(k_ref[pl.ds(start, tq), h * LANES:(h + 1) * LANES], qs[h],
                                (((1,), (1,)), ((), ())), preferred_element_type=F32)
                for h in range(n_heads)]

    def park(scs, ms, diagonal):
        new_ms, alphas = [], []
        for h in range(n_heads):
            sc = jnp.where(causal, scs[h], -jnp.inf) if diagonal else scs[h]
            m_new = jnp.maximum(ms[h], jnp.max(sc, axis=0, keepdims=True))
            alphas.append(jnp.exp2((ms[h] - m_new) * c))
            new_ms.append(m_new)
            sc_ref[h] = sc
        return tuple(new_ms), tuple(alphas)

    def consume(kb, ms, alphas, ls, accs):
        new_ls, new_accs = [], []
        for h in range(n_heads):
            p = jnp.exp2((sc_ref[h] - ms[h]) * c)
            new_ls.append(alphas[h] * ls[h] + jnp.sum(p, axis=0, keepdims=True))
            vt = vt_ref[kb, h * HEAD_DIM:(h + 1) * HEAD_DIM, :]
            new_accs.append(alphas[h] * accs[h]
                            + jnp.dot(vt, p.astype(BF16), preferred_element_type=F32))
        return tuple(new_ls), tuple(new_accs)

    row = lambda v: (jnp.full((1, tq), v, F32),) * n_heads
    ms, alphas = park(score_tiles(qi), row(-jnp.inf), True)
    ls = row(0.0)
    accs = (jnp.zeros((HEAD_DIM, tq), F32),) * n_heads

    def trip(j, carry):
        ms, alphas, ls, accs = carry
        scs = score_tiles(j)
        ls, accs = consume(jnp.where(j == 0, qi, j - 1), ms, alphas, ls, accs)
        ms, alphas = park(scs, ms, False)
        return ms, alphas, ls, accs

    ms, alphas, ls, accs = lax.fori_loop(0, qi, trip, (ms, alphas, ls, accs))
    ls, accs = consume(jnp.where(qi == 0, qi, qi - 1), ms, alphas, ls, accs)
    outs = [accs[h] / ls[h] for h in range(n_heads)]
    for p in range(n_pairs):
        pair = jnp.concatenate([outs[2 * p], outs[2 * p + 1]], axis=0)
        o_ref[:, p * LANES:(p + 1) * LANES] = pair.T.astype(o_ref.dtype)


def _causal_attention(q, k, vt, b, s, tq, n_pairs):
    kern = functools.partial(_causal_attn_kernel, tq=tq, n_pairs=n_pairs)
    wq = 2 * n_pairs * LANES
    wv = n_pairs * LANES
    return pl.pallas_call(
        kern,
        grid=(b, N_PAIRS // n_pairs, s // tq),
        in_specs=[
            pl.BlockSpec((None, tq, wq), lambda bi, g, qi: (bi, qi, g)),
            pl.BlockSpec((None, s, wq), lambda bi, g, qi: (bi, 0, g)),
            pl.BlockSpec((None, s // tq, wv, tq), lambda bi, g, qi: (bi, 0, g, 0)),
        ],
        out_specs=pl.BlockSpec((None, tq, wv), lambda bi, g, qi: (bi, qi, g)),
        out_shape=jax.ShapeDtypeStruct((b, s, MIX_W), BF16),
        scratch_shapes=[pltpu.VMEM((2 * n_pairs, tq, tq), F32)],
        compiler_params=_cparams("parallel", "parallel", "arbitrary"),
        name="causal_attention",
    )(q, k, vt)


def _mem_kv_kernel(kv_ref, g_ref, k_ref, v_ref):
    k = kv_ref[:, :MEM_Q]
    v = kv_ref[:, MEM_Q:]
    g = g_ref[...]
    lane = lax.broadcasted_iota(jnp.int32, k.shape, 1)
    for h in range(N_MEM_HEADS):
        in_head = (lane >= h * HEAD_DIM) & (lane < (h + 1) * HEAD_DIM)
        ms = jnp.sum(jnp.where(in_head, k * k, 0.0), axis=-1, keepdims=True) / HEAD_DIM
        k_ref[h] = jnp.where(in_head, k * lax.rsqrt(ms + EPS) * g, 0.0).astype(k_ref.dtype)
        v_ref[h] = jnp.where(in_head, v, 0.0).astype(v_ref.dtype)


def _mem_out_kernel(mix_ref, qm_ref, k_ref, v_ref, g_ref, w_ref, x_ref, o_ref):
    tm = qm_ref.shape[0]
    rows = [slice(r, r + MEM_ROWS) for r in range(0, tm, MEM_ROWS)]
    outs = [jnp.dot(mix_ref[r, :], w_ref[:MIX_W, :], preferred_element_type=F32) for r in rows]
    lane = lax.broadcasted_iota(jnp.int32, (MEM_ROWS, MEM_Q), 1)
    qns = []
    for r in rows:
        q = qm_ref[r, :]
        inv = jnp.zeros_like(q)
        for h in range(N_MEM_HEADS):
            in_head = (lane >= h * HEAD_DIM) & (lane < (h + 1) * HEAD_DIM)
            ms = jnp.sum(jnp.where(in_head, q * q, 0.0), axis=-1, keepdims=True) / HEAD_DIM
            inv = jnp.where(in_head, lax.rsqrt(ms + EPS), inv)
        qns.append((q * inv * g_ref[...]).astype(BF16))
    scs = [[lax.dot_general(qn, k_ref[h], (((1,), (1,)), ((), ())), preferred_element_type=F32)
            for h in range(N_MEM_HEADS)] for qn in qns]
    ps = []
    for chunk in scs:
        ps.append([])
        for sc in chunk:
            sc = sc * (HEAD_DIM ** -0.5)
            e = jnp.exp(sc - jnp.max(sc, axis=-1, keepdims=True))
            ps[-1].append((e / jnp.sum(e, axis=-1, keepdims=True)).astype(BF16))
    for r, out, p in zip(rows, outs, ps):
        mem_o = jnp.dot(p[0], v_ref[0], preferred_element_type=F32)
        for h in range(1, N_MEM_HEADS):
            mem_o = mem_o + jnp.dot(p[h], v_ref[h], preferred_element_type=F32)
        out = out + jnp.dot(mem_o.astype(BF16), w_ref[MIX_W:, :], preferred_element_type=F32)
        o_ref[r, :] = x_ref[r, :] + out


def _mem_kv_prep(mem2d, ln_g, w_kv, g_kn, b, n_mem):
    w = w_kv.reshape(D_MODEL, N_MEM_HEADS, 2, HEAD_DIM).transpose(0, 2, 1, 3)
    w = w.reshape(D_MODEL, 2 * MEM_Q).astype(BF16)
    (kv,) = _rms_matmul(mem2d, ln_g, w, [(0, 2 * MEM_Q)], [F32], tm=n_mem, name="mem_kv_proj")
    shape = jax.ShapeDtypeStruct((b, N_MEM_HEADS, n_mem, MEM_Q), BF16)
    spec = pl.BlockSpec((None, N_MEM_HEADS, n_mem, MEM_Q), lambda i: (i, 0, 0, 0))
    return pl.pallas_call(
        _mem_kv_kernel,
        grid=(b,),
        in_specs=[pl.BlockSpec((n_mem, 2 * MEM_Q), lambda i: (i, 0)),
                  pl.BlockSpec((1, MEM_Q), lambda i: (0, 0))],
        out_specs=[spec, spec],
        out_shape=[shape, shape],
        compiler_params=_cparams("parallel"),
        name="mem_kv_prep",
    )(kv, jnp.tile(g_kn.astype(F32), N_MEM_HEADS).reshape(1, MEM_Q))


def _mem_out_proj(mix, q_mem, k_mem, v_mem, g_qn, w_out, x, s, tm):
    t = x.shape[0]
    n_mem = k_mem.shape[2]
    per_seq = s // tm
    row = lambda i: (i, 0)
    mem_spec = pl.BlockSpec((None, N_MEM_HEADS, n_mem, MEM_Q), lambda i: (i // per_seq, 0, 0, 0))
    return pl.pallas_call(
        _mem_out_kernel,
        grid=(t // tm,),
        in_specs=[
            pl.BlockSpec((tm, MIX_W), row),
            pl.BlockSpec((tm, MEM_Q), row),
            mem_spec,
            mem_spec,
            pl.BlockSpec((1, MEM_Q), lambda i: (0, 0)),
            pl.BlockSpec((D_MODEL, D_MODEL), lambda i: (0, 0)),
            pl.BlockSpec((tm, D_MODEL), row),
        ],
        out_specs=pl.BlockSpec((tm, D_MODEL), row),
        out_shape=jax.ShapeDtypeStruct((t, D_MODEL), F32),
        compiler_params=_cparams("parallel"),
        name="mem_attn_out_proj",
    )(mix, q_mem, k_mem, v_mem, jnp.tile(g_qn.astype(F32), N_MEM_HEADS).reshape(1, MEM_Q),
      w_out, x)


def _ffn_kernel(x_ref, g_ref, wgu_ref, wd_ref, o_ref, *, tf):
    x = x_ref[...]
    h = _rms_rows(x, g_ref[...]).astype(BF16)
    acc = x
    for c in range(0, D_FF, tf):
        gate = jnp.dot(h, wgu_ref[:, c:c + tf], preferred_element_type=F32)
        up = jnp.dot(h, wgu_ref[:, D_FF + c:D_FF + c + tf], preferred_element_type=F32)
        act = gate * (1.0 / (1.0 + jnp.exp(-gate))) * up
        acc = acc + jnp.dot(act.astype(BF16), wd_ref[c:c + tf, :], preferred_element_type=F32)
    o_ref[...] = acc


def _ffn(x, g, w_gu, w_down, tm, tf):
    t = x.shape[0]
    once = pl.Buffered(1)
    return pl.pallas_call(
        functools.partial(_ffn_kernel, tf=tf),
        grid=(t // tm,),
        in_specs=[
            pl.BlockSpec((tm, D_MODEL), lambda i: (i, 0)),
            pl.BlockSpec((1, D_MODEL), lambda i: (0, 0)),
            pl.BlockSpec((D_MODEL, 2 * D_FF), lambda i: (0, 0), pipeline_mode=once),
            pl.BlockSpec((D_FF, D_MODEL), lambda i: (0, 0), pipeline_mode=once),
        ],
        out_specs=pl.BlockSpec((tm, D_MODEL), lambda i: (i, 0)),
        out_shape=jax.ShapeDtypeStruct((t, D_MODEL), F32),
        compiler_params=_cparams("parallel"),
        name="swiglu_ffn",
    )(x, g.reshape(1, D_MODEL), w_gu, w_down)


def kernel(x, mem, positions, ln_attn, w_out, ln_mem, w_mem_kv, g_qn_mem, g_kn_mem, ln_ffn,
           w_ffn_gu, w_ffn_down, sb_w_in, mla_w_in, mla_q_norm, mla_w_q_up, mla_kv_norm,
           mla_w_kv_up, mla_g_qn, mla_g_kn):
    b, s, d = x.shape
    n_mem = mem.shape[1]
    t = b * s
    x2 = x.reshape(t, d)
    mem2 = mem.reshape(b * n_mem, d)

    qk, q_mem, vt = _sb_in_proj(x2, ln_attn[0], sb_w_in[0], b, s, tm=512, tk=256)
    mix = _sb_attention(qk.reshape(b, s, 2 * MIX_W), vt, b, s, tq=256, n_pairs=2).reshape(t, MIX_W)
    k_mem, v_mem = _mem_kv_prep(mem2, ln_mem[0], w_mem_kv[0], g_kn_mem[0], b, n_mem)
    x2 = _mem_out_proj(mix, q_mem, k_mem, v_mem, g_qn_mem[0], w_out[0].astype(BF16), x2, s, tm=1024)
    x2 = _ffn(x2, ln_ffn[0], w_ffn_gu[0].astype(BF16), w_ffn_down[0].astype(BF16), tm=1024, tf=256)

    w_in = mla_w_in[0]
    o_kr = MLA_Q_RANK + MLA_KV_RANK
    o_qm = o_kr + MLA_ROPE
    w_kr = jnp.concatenate([jnp.zeros((D_MODEL, MLA_NOPE), F32), w_in[:, o_kr:o_qm],
                            w_in[:, o_kr:o_qm]], axis=1)
    w_in = jnp.concatenate([w_in[:, :o_kr], w_in[:, o_qm:], w_kr], axis=1).astype(BF16)
    c_q, c_kv, q_mem, k_rope = _rms_matmul(
        x2, ln_attn[1], w_in,
        [(0, MLA_Q_RANK), (MLA_Q_RANK, MLA_KV_RANK), (o_kr, MEM_Q), (o_kr + MEM_Q, LANES)],
        [F32, F32, F32, F32], tm=512, name="mla_in_proj")
    cosf, sinf = _rope_tables(positions, tm=512)
    q, k, vt = _mla_prep(c_q, c_kv, k_rope, cosf, sinf, mla_q_norm[0], mla_w_q_up[0],
                         mla_kv_norm[0], mla_w_kv_up[0], mla_g_qn[0], mla_g_kn[0], b, s,
                         tm=512, tk=256)
    mix = _causal_attention(q.reshape(b, s, -1), k.reshape(b, s, -1), vt,
                            b, s, tq=256, n_pairs=2).reshape(t, MIX_W)
    k_mem, v_mem = _mem_kv_prep(mem2, ln_mem[1], w_mem_kv[1], g_kn_mem[1], b, n_mem)
    x2 = _mem_out_proj(mix, q_mem, k_mem, v_mem, g_qn_mem[1], w_out[1].astype(BF16), x2, s, tm=1024)
    x2 = _ffn(x2, ln_ffn[1], w_ffn_gu[1].astype(BF16), w_ffn_down[1].astype(BF16), tm=1024, tf=256)
    return x2.reshape(b, s, d)
```

```python
import functools

import jax
import jax.numpy as jnp
from jax import lax
from jax.experimental import pallas as pl
from jax.experimental.pallas import tpu as pltpu

F32 = jnp.float32
BF16 = jnp.bfloat16

D_MODEL = 1024
HEAD_DIM = 64
N_HEADS = 12
N_PAIRS = N_HEADS // 2
MIX_W = N_HEADS * HEAD_DIM
MLA_Q_RANK = 768
MLA_KV_RANK = 256
MLA_NOPE = 64
MLA_ROPE = 32
MLA_QK = MLA_NOPE + MLA_ROPE
N_MEM_HEADS = 4
MEM_Q = N_MEM_HEADS * HEAD_DIM
D_FF = 2816
ROPE_THETA = 10000.0
EPS = 1e-6
LOG2_E = 1.4426950408889634
SOFTMAX_C = MLA_QK ** -0.5 * LOG2_E
SUM_ROWS = 16
MEM_ROWS = 256
LANES = 128

VMEM_LIMIT = 56 * 1024 * 1024


def _cparams(*sem):
    return pltpu.CompilerParams(dimension_semantics=sem, vmem_limit_bytes=VMEM_LIMIT)


def _rms_rows(x, g):
    ms = jnp.mean(x * x, axis=-1, keepdims=True)
    return x * lax.rsqrt(ms + EPS) * g


def _rms_matmul_kernel(x_ref, g_ref, w_ref, *o_refs, segments, chunk):
    h = _rms_rows(x_ref[...].astype(F32), g_ref[...]).astype(BF16)
    for o_ref, (start, width) in zip(o_refs, segments):
        for c in range(0, width, chunk):
            cw = min(chunk, width - c)
            acc = jnp.dot(h, w_ref[:, start + c:start + c + cw], preferred_element_type=F32)
            o_ref[:, c:c + cw] = acc.astype(o_ref.dtype)


def _rms_matmul(x, g, w, segments, dtypes, tm, name):
    t, k = x.shape
    n = w.shape[1]
    kern = functools.partial(_rms_matmul_kernel, segments=tuple(segments), chunk=512)
    return pl.pallas_call(
        kern,
        grid=(t // tm,),
        in_specs=[
            pl.BlockSpec((tm, k), lambda i: (i, 0)),
            pl.BlockSpec((1, k), lambda i: (0, 0)),
            pl.BlockSpec((k, n), lambda i: (0, 0)),
        ],
        out_specs=[pl.BlockSpec((tm, wd), lambda i: (i, 0)) for _, wd in segments],
        out_shape=[jax.ShapeDtypeStruct((t, wd), dt) for (_, wd), dt in zip(segments, dtypes)],
        compiler_params=_cparams("parallel"),
        name=name,
    )(x, g.reshape(1, k), w)


N_SPLIT = 2
EXP_UNDERFLOW = -104.0
U_EXTRA = 16


def _sb_in_proj_kernel(x_ref, g_ref, w_ref, wvt_ref, qk_ref, qm_ref, vt_ref):
    h = _rms_rows(x_ref[...], g_ref[...]).astype(BF16)
    n_qk = qk_ref.shape[1]
    for c in range(0, n_qk, 512):
        qk_ref[:, c:c + 512] = jnp.dot(h, w_ref[:, c:c + 512],
                                       preferred_element_type=F32).astype(qk_ref.dtype)
    qm_ref[...] = jnp.dot(h, w_ref[:, n_qk:], preferred_element_type=F32)
    vt = lax.dot_general(wvt_ref[...], h, (((1,), (1,)), ((), ())), preferred_element_type=F32)
    tk = vt_ref.shape[-1]
    for c in range(vt_ref.shape[0]):
        vt_ref[c] = vt[:, c * tk:(c + 1) * tk].astype(vt_ref.dtype)


def _sb_in_proj(x, g, w_in, b, s, tm, tk):
    t = x.shape[0]
    w_qk_qm = jnp.concatenate([w_in[:, :2 * MIX_W], w_in[:, 3 * MIX_W:]], axis=1).astype(BF16)
    wvt = w_in[:, 2 * MIX_W:3 * MIX_W].T.astype(BF16)
    per_seq = s // tm
    row = lambda i: (i, 0)
    fixed = lambda i: (0, 0)
    return pl.pallas_call(
        _sb_in_proj_kernel,
        grid=(t // tm,),
        in_specs=[
            pl.BlockSpec((tm, D_MODEL), row),
            pl.BlockSpec((1, D_MODEL), fixed),
            pl.BlockSpec((D_MODEL, 2 * MIX_W + MEM_Q), fixed),
            pl.BlockSpec((MIX_W, D_MODEL), fixed),
        ],
        out_specs=[pl.BlockSpec((tm, 2 * MIX_W), row), pl.BlockSpec((tm, MEM_Q), row),
                   pl.BlockSpec((None, tm // tk, MIX_W, tk),
                                lambda i: (i // per_seq, i % per_seq, 0, 0))],
        out_shape=[jax.ShapeDtypeStruct((t, 2 * MIX_W), BF16),
                   jax.ShapeDtypeStruct((t, MEM_Q), F32),
                   jax.ShapeDtypeStruct((b, s // tk, MIX_W, tk), BF16)],
        compiler_params=_cparams("parallel"),
        name="sb_in_proj",
    )(x, g.reshape(1, D_MODEL), w_qk_qm, wvt)


def _split_rows(x):
    parts = []
    r = x
    for i in range(N_SPLIT):
        piece = r.astype(BF16)
        parts.append(piece)
        if i + 1 < N_SPLIT:
            r = r - piece.astype(F32)
    return jnp.concatenate(parts, axis=0)


def _sb_attn_kernel(q_ref, k_ref, vt_ref, u_ref, o_ref, *, tq, n_pairs):
    qi = pl.program_id(2)
    lane = lax.broadcasted_iota(jnp.int32, (tq, LANES), 1)
    key = lax.broadcasted_iota(jnp.int32, (tq, tq), 0)
    qry = lax.broadcasted_iota(jnp.int32, (tq, tq), 1)
    strict = key < qry
    u = u_ref[...]
    heads = [(p, h) for p in range(n_pairs) for h in range(2)]
    n = len(heads)
    qs = []
    for p, h in heads:
        in_head = (lane >= h * HEAD_DIM) & (lane < (h + 1) * HEAD_DIM)
        q_pair = q_ref[:, p * LANES:(p + 1) * LANES]
        qs.append(jnp.where(in_head, q_pair * (HEAD_DIM ** -0.5), 0).astype(BF16))

    def score_stage(kb, diagonal):
        start = pl.multiple_of(kb * tq, tq)
        zs = []
        for idx in range(n):
            p = heads[idx][0]
            k = k_ref[pl.ds(start, tq), p * LANES:(p + 1) * LANES]
            zs.append(lax.dot_general(k, qs[idx], (((1,), (1,)), ((), ())),
                                      preferred_element_type=F32))
        log_betas, log_fails = [], []
        for z in zs:
            soft = jnp.log(1.0 + jnp.exp(-jnp.abs(z)))
            log_beta = jnp.minimum(z, 0.0) - soft
            log_fail = log_beta - z
            if diagonal:
                log_fail = jnp.where(strict, log_fail, 0.0)
            log_betas.append(log_beta)
            log_fails.append(log_fail)
        return log_betas, log_fails

    def sum_stage(log_fails):
        return [jnp.dot(u, _split_rows(lf), preferred_element_type=F32) for lf in log_fails]

    def value_stage(kb, log_betas, sums, state, diagonal):
        ws = []
        for idx in range(n):
            w = jnp.exp(log_betas[idx] + sums[idx][:tq] + state[idx][0])
            if diagonal:
                w = jnp.where(strict, w, 0.0)
            ws.append(w.astype(BF16))
        new = []
        for idx in range(n):
            vt = vt_ref[kb, idx * HEAD_DIM:(idx + 1) * HEAD_DIM, :]
            acc = state[idx][1] + jnp.dot(vt, ws[idx], preferred_element_type=F32)
            carry = state[idx][0] + sums[idx][tq:tq + 1]
            new.append((carry, acc))
        return tuple(new)

    def block(kb, state):
        log_betas, log_fails = score_stage(kb, False)
        return value_stage(kb, log_betas, sum_stage(log_fails), state, False)

    zero = ((jnp.zeros((1, tq), F32), jnp.zeros((HEAD_DIM, tq), F32)),) * n

    def diagonal_only():
        log_betas, log_fails = score_stage(qi, True)
        return value_stage(qi, log_betas, sum_stage(log_fails), zero, True)

    def diagonal_and_previous():
        lb_d, lf_d = score_stage(qi, True)
        lb_p, lf_p = score_stage(qi - 1, False)
        sums_d = sum_stage(lf_d)
        sums_p = sum_stage(lf_p)
        state = value_stage(qi, lb_d, sums_d, zero, True)
        return value_stage(qi - 1, lb_p, sums_p, state, False)

    def live(state):
        top = state[0][0]
        for carry, _ in state[1:]:
            top = jnp.maximum(top, carry)
        return jnp.max(top) >= EXP_UNDERFLOW

    state = lax.cond(qi > 0, diagonal_and_previous, diagonal_only)

    def step(loop):
        i, _, state = loop
        state = block(qi - 1 - i, state)
        return i + 1, live(state), state

    _, _, state = lax.while_loop(lambda loop: (loop[0] < qi) & loop[1], step,
                                 (jnp.int32(1), live(state), state))
    for p in range(n_pairs):
        pair = jnp.concatenate([state[2 * p][1], state[2 * p + 1][1]], axis=0)
        o_ref[:, p * LANES:(p + 1) * LANES] = pair.T.astype(o_ref.dtype)


def _sb_attention(qk, vt, b, s, tq, n_pairs):
    r = jnp.arange(tq)
    u = (r[None, :] > r[:, None]).astype(BF16)
    u = jnp.concatenate([u, jnp.ones((U_EXTRA, tq), BF16)], axis=0)
    u = jnp.concatenate([u] * N_SPLIT, axis=1)
    kern = functools.partial(_sb_attn_kernel, tq=tq, n_pairs=n_pairs)
    w = n_pairs * LANES
    groups = N_PAIRS // n_pairs
    return pl.pallas_call(
        kern,
        grid=(b, groups, s // tq),
        in_specs=[
            pl.BlockSpec((None, tq, w), lambda bi, g, qi: (bi, qi, g)),
            pl.BlockSpec((None, s, w), lambda bi, g, qi: (bi, 0, groups + g)),
            pl.BlockSpec((None, s // tq, w, tq), lambda bi, g, qi: (bi, 0, g, 0)),
            pl.BlockSpec((tq + U_EXTRA, N_SPLIT * tq), lambda bi, g, qi: (0, 0)),
        ],
        out_specs=pl.BlockSpec((None, tq, w), lambda bi, g, qi: (bi, qi, g)),
        out_shape=jax.ShapeDtypeStruct((b, s, MIX_W), BF16),
        compiler_params=_cparams("parallel", "parallel", "arbitrary"),
        name="sb_attention",
    )(qk, qk, vt, u)


def _rope_table_kernel(pos_ref, invf_ref, sign_ref, keep_ref, cos_ref, sin_ref):
    ang = pos_ref[...].astype(F32) * invf_ref[...]
    cos_ref[...] = jnp.cos(ang) * keep_ref[...]
    sin_ref[...] = jnp.sin(ang) * sign_ref[...]


ROPE_HALF = MLA_ROPE // 2
ROT_LO = MLA_NOPE + ROPE_HALF
ROT_HI = ROT_LO + MLA_ROPE


def _rope_tables(positions, tm):
    t = positions.size
    inv_freq = ROPE_THETA ** (-jnp.arange(ROPE_HALF, dtype=F32) / ROPE_HALF)
    zeros = lambda n: jnp.zeros((n,), F32)
    ones = lambda n: jnp.ones((n,), F32)
    invf = jnp.concatenate([zeros(ROT_LO), inv_freq, inv_freq, zeros(LANES - ROT_HI)])
    sign = jnp.concatenate([zeros(ROT_LO), ones(ROPE_HALF), -ones(ROPE_HALF),
                            zeros(LANES - ROT_HI)])
    keep = jnp.concatenate([ones(MLA_NOPE), zeros(ROPE_HALF), ones(MLA_ROPE),
                            zeros(LANES - ROT_HI)])
    fixed = pl.BlockSpec((1, LANES), lambda i: (0, 0))
    return pl.pallas_call(
        _rope_table_kernel,
        grid=(t // tm,),
        in_specs=[pl.BlockSpec((tm, 1), lambda i: (i, 0)), fixed, fixed, fixed],
        out_specs=[pl.BlockSpec((tm, LANES), lambda i: (i, 0))] * 2,
        out_shape=[jax.ShapeDtypeStruct((t, LANES), F32)] * 2,
        compiler_params=_cparams("parallel"),
        name="rope_tables",
    )(positions.reshape(t, 1), invf.reshape(1, LANES), sign.reshape(1, LANES),
      keep.reshape(1, LANES))


def _head_norm_rope(x, g, real, cosf, sinf):
    ms = jnp.sum(x * x * real, axis=-1, keepdims=True) / MLA_QK
    y = x * lax.rsqrt(ms + EPS) * g
    return y * cosf + pltpu.roll(y, ROPE_HALF, 1) * sinf


def _mla_q_kernel(cq_ref, gq_ref, w_ref, gh_ref, real_ref, cos_ref, sin_ref, q_ref):
    h = _rms_rows(cq_ref[...], gq_ref[...]).astype(BF16)
    cosf = cos_ref[...]
    sinf = sin_ref[...]
    g = gh_ref[...]
    real = real_ref[...]
    for hd in range(N_HEADS):
        sl = slice(hd * LANES, (hd + 1) * LANES)
        x = jnp.dot(h, w_ref[:, sl], preferred_element_type=F32)
        q_ref[:, sl] = (_head_norm_rope(x, g, real, cosf, sinf) * SOFTMAX_C).astype(q_ref.dtype)


def _mla_k_kernel(ckv_ref, gkv_ref, w_ref, wvt_ref, kr_ref, gh_ref, real_ref, cos_ref, sin_ref,
                  k_ref, vt_ref):
    h = _rms_rows(ckv_ref[...], gkv_ref[...]).astype(BF16)
    cosf = cos_ref[...]
    sinf = sin_ref[...]
    g = gh_ref[...]
    real = real_ref[...]
    k_rope = kr_ref[...]
    for hd in range(N_HEADS):
        sl = slice(hd * LANES, (hd + 1) * LANES)
        x = jnp.dot(h, w_ref[:, sl], preferred_element_type=F32) + k_rope
        k_ref[:, sl] = _head_norm_rope(x, g, real, cosf, sinf).astype(k_ref.dtype)
    vt = lax.dot_general(wvt_ref[...], h, (((1,), (1,)), ((), ())), preferred_element_type=F32)
    tk = vt_ref.shape[-1]
    for c in range(vt_ref.shape[0]):
        vt_ref[c] = vt[:, c * tk:(c + 1) * tk].astype(vt_ref.dtype)


def _pad_head_gain(g):
    g = g.astype(F32)
    return jnp.concatenate([g, g[MLA_NOPE:]]).reshape(1, LANES)


def _mla_prep(c_q, c_kv, k_rope, cosf, sinf, q_norm, w_q_up, kv_norm, w_kv_up, g_qn, g_kn,
              b, s, tm, tk):
    t = c_q.shape[0]
    wq = w_q_up.reshape(MLA_Q_RANK, N_HEADS, MLA_QK)
    wq = jnp.concatenate([wq, wq[:, :, MLA_NOPE:]], axis=2)
    wq = wq.reshape(MLA_Q_RANK, N_HEADS * LANES).astype(BF16)
    real = (jnp.arange(LANES) < MLA_QK).astype(F32).reshape(1, LANES)
    wkv = w_kv_up.reshape(MLA_KV_RANK, N_HEADS, MLA_NOPE + HEAD_DIM)
    wk = jnp.pad(wkv[:, :, :MLA_NOPE], ((0, 0), (0, 0), (0, LANES - MLA_NOPE)))
    wk = wk.reshape(MLA_KV_RANK, N_HEADS * LANES).astype(BF16)
    wvt = wkv[:, :, MLA_NOPE:].reshape(MLA_KV_RANK, MIX_W).T.astype(BF16)

    row = lambda i: (i, 0)
    fixed = lambda i: (0, 0)
    q = pl.pallas_call(
        _mla_q_kernel,
        grid=(t // tm,),
        in_specs=[
            pl.BlockSpec((tm, MLA_Q_RANK), row),
            pl.BlockSpec((1, MLA_Q_RANK), fixed),
            pl.BlockSpec((MLA_Q_RANK, N_HEADS * LANES), fixed),
            pl.BlockSpec((1, LANES), fixed),
            pl.BlockSpec((1, LANES), fixed),
            pl.BlockSpec((tm, LANES), row),
            pl.BlockSpec((tm, LANES), row),
        ],
        out_specs=pl.BlockSpec((tm, N_HEADS * LANES), row),
        out_shape=jax.ShapeDtypeStruct((t, N_HEADS * LANES), BF16),
        compiler_params=_cparams("parallel"),
        name="mla_q_prep",
    )(c_q, q_norm.reshape(1, -1), wq, _pad_head_gain(g_qn), real, cosf, sinf)
    per_seq = s // tm
    k, vt = pl.pallas_call(
        _mla_k_kernel,
        grid=(t // tm,),
        in_specs=[
            pl.BlockSpec((tm, MLA_KV_RANK), row),
            pl.BlockSpec((1, MLA_KV_RANK), fixed),
            pl.BlockSpec((MLA_KV_RANK, N_HEADS * LANES), fixed),
            pl.BlockSpec((MIX_W, MLA_KV_RANK), fixed),
            pl.BlockSpec((tm, LANES), row),
            pl.BlockSpec((1, LANES), fixed),
            pl.BlockSpec((1, LANES), fixed),
            pl.BlockSpec((tm, LANES), row),
            pl.BlockSpec((tm, LANES), row),
        ],
        out_specs=[pl.BlockSpec((tm, N_HEADS * LANES), row),
                   pl.BlockSpec((None, tm // tk, MIX_W, tk),
                                lambda i: (i // per_seq, i % per_seq, 0, 0))],
        out_shape=[jax.ShapeDtypeStruct((t, N_HEADS * LANES), BF16),
                   jax.ShapeDtypeStruct((b, s // tk, MIX_W, tk), BF16)],
        compiler_params=_cparams("parallel"),
        name="mla_kv_prep",
    )(c_kv, kv_norm.reshape(1, -1), wk, wvt, k_rope, _pad_head_gain(g_kn), real, cosf, sinf)
    return q, k, vt


def _causal_attn_kernel(q_ref, k_ref, vt_ref, o_ref, sa_ref, sb_ref, *, tq, n_pairs):
    qi = pl.program_id(2)
    key = lax.broadcasted_iota(jnp.int32, (tq, tq), 0)
    qry = lax.broadcasted_iota(jnp.int32, (tq, tq), 1)
    causal = key <= qry
    n_heads = 2 * n_pairs
    qs = [q_ref[:, h * LANES:(h + 1) * LANES] for h in range(n_heads)]

    def score_tiles(kb):
        start = pl.multiple_of(kb * tq, tq)
        return [lax.dot_general(k_ref[pl.ds(start, tq), h * LANES:(h + 1) * LANES], qs[h],
                                (((1,), (1,)), ((), ())), preferred_element_type=F32)
                for h in range(n_heads)]

    ones = jnp.ones((SUM_ROWS, tq), BF16)

    def park(buf, scs, ms, diagonal):
        new_ms, alphas = [], []
        for h in range(n_heads):
            sc = jnp.where(causal, scs[h], -jnp.inf) if diagonal else scs[h]
            m_new = jnp.maximum(ms[h], jnp.max(sc, axis=0, keepdims=True))
            alphas.append(jnp.exp2(ms[h] - m_new))
            new_ms.append(m_new)
            buf[h] = sc
        return tuple(new_ms), tuple(alphas)

    def consume(buf, kb, ms, alphas, accs):
        new_accs = []
        for h in range(n_heads):
            p = jnp.exp2(buf[h] - ms[h]).astype(BF16)
            vt = jnp.concatenate([vt_ref[kb, h * HEAD_DIM:(h + 1) * HEAD_DIM, :], ones], axis=0)
            new_accs.append(alphas[h] * accs[h] + jnp.dot(vt, p, preferred_element_type=F32))
        return tuple(new_accs)

    row = lambda v: (jnp.full((1, tq), v, F32),) * n_heads
    ms, alphas = park(sb_ref, score_tiles(qi), row(-jnp.inf), True)
    accs = (jnp.zeros((HEAD_DIM + SUM_ROWS, tq), F32),) * n_heads

    def double(t, carry):
        ms, alphas, accs = carry
        j = 2 * t
        scs_a = score_tiles(j)
        scs_b = score_tiles(j + 1)
        ms_a, alphas_a = park(sa_ref, scs_a, ms, False)
        accs = consume(sb_ref, jnp.where(j == 0, qi, j - 1), ms, alphas, accs)
        accs = consume(sa_ref, j, ms_a, alphas_a, accs)
        ms, alphas = park(sb_ref, scs_b, ms_a, False)
        return ms, alphas, accs

    ms, alphas, accs = lax.fori_loop(0, qi // 2, double, (ms, alphas, accs))
    before_last = jnp.where(qi <= 1, qi, qi - 2)

    def odd_tail(ms, alphas, accs):
        ms_a, alphas_a = park(sa_ref, score_tiles(qi - 1), ms, False)
        accs = consume(sb_ref, before_last, ms, alphas, accs)
        return consume(sa_ref, qi - 1, ms_a, alphas_a, accs)

    def even_tail(ms, alphas, accs):
        return consume(sb_ref, jnp.where(qi == 0, qi, qi - 1), ms, alphas, accs)

    accs = lax.cond(qi % 2 == 1, odd_tail, even_tail, ms, alphas, accs)
    outs = [acc[:HEAD_DIM] / acc[HEAD_DIM:HEAD_DIM + 1] for acc in accs]
    for p in range(n_pairs):
        pair = jnp.concatenate([outs[2 * p], outs[2 * p + 1]], axis=0)
        o_ref[:, p * LANES:(p + 1) * LANES] = pair.T.astype(o_ref.dtype)


def _causal_attention(q, k, vt, b, s, tq, n_pairs):
    kern = functools.partial(_causal_attn_kernel, tq=tq, n_pairs=n_pairs)
    wq = 2 * n_pairs * LANES
    wv = n_pairs * LANES
    return pl.pallas_call(
        kern,
        grid=(b, N_PAIRS // n_pairs, s // tq),
        in_specs=[
            pl.BlockSpec((None, tq, wq), lambda bi, g, qi: (bi, qi, g)),
            pl.BlockSpec((None, s, wq), lambda bi, g, qi: (bi, 0, g)),
            pl.BlockSpec((None, s // tq, wv, tq), lambda bi, g, qi: (bi, 0, g, 0)),
        ],
        out_specs=pl.BlockSpec((None, tq, wv), lambda bi, g, qi: (bi, qi, g)),
        out_shape=jax.ShapeDtypeStruct((b, s, MIX_W), BF16),
        scratch_shapes=[pltpu.VMEM((2 * n_pairs, tq, tq), F32)] * 2,
        compiler_params=_cparams("parallel", "parallel", "arbitrary"),
        name="causal_attention",
    )(q, k, vt)


def _mem_kv_kernel(kv_ref, g_ref, k_ref, v_ref):
    k = kv_ref[:, :MEM_Q]
    v = kv_ref[:, MEM_Q:]
    g = g_ref[...]
    lane = lax.broadcasted_iota(jnp.int32, k.shape, 1)
    for h in range(N_MEM_HEADS):
        in_head = (lane >= h * HEAD_DIM) & (lane < (h + 1) * HEAD_DIM)
        ms = jnp.sum(jnp.where(in_head, k * k, 0.0), axis=-1, keepdims=True) / HEAD_DIM
        k_ref[h] = jnp.where(in_head, k * lax.rsqrt(ms + EPS) * g, 0.0).astype(k_ref.dtype)
        v_ref[h] = jnp.where(in_head, v, 0.0).astype(v_ref.dtype)


def _mem_out_kernel(mix_ref, qm_ref, k_ref, v_ref, g_ref, w_ref, x_ref, o_ref):
    tm = qm_ref.shape[0]
    rows = [slice(r, r + MEM_ROWS) for r in range(0, tm, MEM_ROWS)]
    outs = [jnp.dot(mix_ref[r, :], w_ref[:MIX_W, :], preferred_element_type=F32) for r in rows]
    lane = lax.broadcasted_iota(jnp.int32, (MEM_ROWS, MEM_Q), 1)
    qns = []
    for r in rows:
        q = qm_ref[r, :]
        inv = jnp.zeros_like(q)
        for h in range(N_MEM_HEADS):
            in_head = (lane >= h * HEAD_DIM) & (lane < (h + 1) * HEAD_DIM)
            ms = jnp.sum(jnp.where(in_head, q * q, 0.0), axis=-1, keepdims=True) / HEAD_DIM
            inv = jnp.where(in_head, lax.rsqrt(ms + EPS), inv)
        qns.append((q * inv * g_ref[...]).astype(BF16))
    scs = [[lax.dot_general(qn, k_ref[h], (((1,), (1,)), ((), ())), preferred_element_type=F32)
            for h in range(N_MEM_HEADS)] for qn in qns]
    ps = []
    for chunk in scs:
        ps.append([])
        for sc in chunk:
            sc = sc * (HEAD_DIM ** -0.5)
            e = jnp.exp(sc - jnp.max(sc, axis=-1, keepdims=True))
            ps[-1].append((e / jnp.sum(e, axis=-1, keepdims=True)).astype(BF16))
    for r, out, p in zip(rows, outs, ps):
        mem_o = jnp.dot(p[0], v_ref[0], preferred_element_type=F32)
        for h in range(1, N_MEM_HEADS):
            mem_o = mem_o + jnp.dot(p[h], v_ref[h], preferred_element_type=F32)
        out = out + jnp.dot(mem_o.astype(BF16), w_ref[MIX_W:, :], preferred_element_type=F32)
        o_ref[r, :] = x_ref[r, :] + out


def _mem_kv_prep(mem2d, ln_g, w_kv, g_kn, b, n_mem):
    w = w_kv.reshape(D_MODEL, N_MEM_HEADS, 2, HEAD_DIM).transpose(0, 2, 1, 3)
    w = w.reshape(D_MODEL, 2 * MEM_Q).astype(BF16)
    (kv,) = _rms_matmul(mem2d, ln_g, w, [(0, 2 * MEM_Q)], [F32], tm=n_mem, name="mem_kv_proj")
    shape = jax.ShapeDtypeStruct((b, N_MEM_HEADS, n_mem, MEM_Q), BF16)
    spec = pl.BlockSpec((None, N_MEM_HEADS, n_mem, MEM_Q), lambda i: (i, 0, 0, 0))
    return pl.pallas_call(
        _mem_kv_kernel,
        grid=(b,),
        in_specs=[pl.BlockSpec((n_mem, 2 * MEM_Q), lambda i: (i, 0)),
                  pl.BlockSpec((1, MEM_Q), lambda i: (0, 0))],
        out_specs=[spec, spec],
        out_shape=[shape, shape],
        compiler_params=_cparams("parallel"),
        name="mem_kv_prep",
    )(kv, jnp.tile(g_kn.astype(F32), N_MEM_HEADS).reshape(1, MEM_Q))


def _mem_out_proj(mix, q_mem, k_mem, v_mem, g_qn, w_out, x, s, tm):
    t = x.shape[0]
    n_mem = k_mem.shape[2]
    per_seq = s // tm
    row = lambda i: (i, 0)
    mem_spec = pl.BlockSpec((None, N_MEM_HEADS, n_mem, MEM_Q), lambda i: (i // per_seq, 0, 0, 0))
    return pl.pallas_call(
        _mem_out_kernel,
        grid=(t // tm,),
        in_specs=[
            pl.BlockSpec((tm, MIX_W), row),
            pl.BlockSpec((tm, MEM_Q), row),
            mem_spec,
            mem_spec,
            pl.BlockSpec((1, MEM_Q), lambda i: (0, 0)),
            pl.BlockSpec((D_MODEL, D_MODEL), lambda i: (0, 0)),
            pl.BlockSpec((tm, D_MODEL), row),
        ],
        out_specs=pl.BlockSpec((tm, D_MODEL), row),
        out_shape=jax.ShapeDtypeStruct((t, D_MODEL), F32),
        compiler_params=_cparams("parallel"),
        name="mem_attn_out_proj",
    )(mix, q_mem, k_mem, v_mem, jnp.tile(g_qn.astype(F32), N_MEM_HEADS).reshape(1, MEM_Q),
      w_out, x)


def _ffn_kernel(x_ref, g_ref, wgu_ref, wd_ref, o_ref, *, tf):
    x = x_ref[...]
    h = _rms_rows(x, g_ref[...]).astype(BF16)
    acc = x
    for c in range(0, D_FF, tf):
        gate = jnp.dot(h, wgu_ref[:, c:c + tf], preferred_element_type=F32)
        up = jnp.dot(h, wgu_ref[:, D_FF + c:D_FF + c + tf], preferred_element_type=F32)
        act = gate * (1.0 / (1.0 + jnp.exp(-gate))) * up
        acc = acc + jnp.dot(act.astype(BF16), wd_ref[c:c + tf, :], preferred_element_type=F32)
    o_ref[...] = acc


def _ffn(x, g, w_gu, w_down, tm, tf):
    t = x.shape[0]
    once = pl.Buffered(1)
    return pl.pallas_call(
        functools.partial(_ffn_kernel, tf=tf),
        grid=(t // tm,),
        in_specs=[
            pl.BlockSpec((tm, D_MODEL), lambda i: (i, 0)),
            pl.BlockSpec((1, D_MODEL), lambda i: (0, 0)),
            pl.BlockSpec((D_MODEL, 2 * D_FF), lambda i: (0, 0), pipeline_mode=once),
            pl.BlockSpec((D_FF, D_MODEL), lambda i: (0, 0), pipeline_mode=once),
        ],
        out_specs=pl.BlockSpec((tm, D_MODEL), lambda i: (i, 0)),
        out_shape=jax.ShapeDtypeStruct((t, D_MODEL), F32),
        compiler_params=_cparams("parallel"),
        name="swiglu_ffn",
    )(x, g.reshape(1, D_MODEL), w_gu, w_down)


def kernel(x, mem, positions, ln_attn, w_out, ln_mem, w_mem_kv, g_qn_mem, g_kn_mem, ln_ffn,
           w_ffn_gu, w_ffn_down, sb_w_in, mla_w_in, mla_q_norm, mla_w_q_up, mla_kv_norm,
           mla_w_kv_up, mla_g_qn, mla_g_kn):
    b, s, d = x.shape
    n_mem = mem.shape[1]
    t = b * s
    x2 = x.reshape(t, d)
    mem2 = mem.reshape(b * n_mem, d)

    qk, q_mem, vt = _sb_in_proj(x2, ln_attn[0], sb_w_in[0], b, s, tm=512, tk=256)
    mix = _sb_attention(qk.reshape(b, s, 2 * MIX_W), vt, b, s, tq=256, n_pairs=2).reshape(t, MIX_W)
    k_mem, v_mem = _mem_kv_prep(mem2, ln_mem[0], w_mem_kv[0], g_kn_mem[0], b, n_mem)
    x2 = _mem_out_proj(mix, q_mem, k_mem, v_mem, g_qn_mem[0], w_out[0].astype(BF16), x2, s, tm=1024)
    x2 = _ffn(x2, ln_ffn[0], w_ffn_gu[0].astype(BF16), w_ffn_down[0].astype(BF16), tm=1024, tf=256)

    w_in = mla_w_in[0]
    o_kr = MLA_Q_RANK + MLA_KV_RANK
    o_qm = o_kr + MLA_ROPE
    w_kr = jnp.concatenate([jnp.zeros((D_MODEL, MLA_NOPE), F32), w_in[:, o_kr:o_qm],
                            w_in[:, o_kr:o_qm]], axis=1)
    w_in = jnp.concatenate([w_in[:, :o_kr], w_in[:, o_qm:], w_kr], axis=1).astype(BF16)
    c_q, c_kv, q_mem, k_rope = _rms_matmul(
        x2, ln_attn[1], w_in,
        [(0, MLA_Q_RANK), (MLA_Q_RANK, MLA_KV_RANK), (o_kr, MEM_Q), (o_kr + MEM_Q, LANES)],
        [F32, F32, F32, F32], tm=512, name="mla_in_proj")
    cosf, sinf = _rope_tables(positions, tm=512)
    q, k, vt = _mla_prep(c_q, c_kv, k_rope, cosf, sinf, mla_q_norm[0], mla_w_q_up[0],
                         mla_kv_norm[0], mla_w_kv_up[0], mla_g_qn[0], mla_g_kn[0], b, s,
                         tm=512, tk=256)
    mix = _causal_attention(q.reshape(b, s, -1), k.reshape(b, s, -1), vt,
                            b, s, tq=256, n_pairs=2).reshape(t, MIX_W)
    k_mem, v_mem = _mem_kv_prep(mem2, ln_mem[1], w_mem_kv[1], g_kn_mem[1], b, n_mem)
    x2 = _mem_out_proj(mix, q_mem, k_mem, v_mem, g_qn_mem[1], w_out[1].astype(BF16), x2, s, tm=1024)
    x2 = _ffn(x2, ln_ffn[1], w_ffn_gu[1].astype(BF16), w_ffn_down[1].astype(BF16), tm=1024, tf=256)
    return x2.reshape(b, s, d)
```

```python
import functools

import jax
import jax.numpy as jnp
from jax import lax
from jax.experimental import pallas as pl
from jax.experimental.pallas import tpu as pltpu

F32 = jnp.float32
BF16 = jnp.bfloat16

D_MODEL = 1024
HEAD_DIM = 64
N_HEADS = 12
N_PAIRS = N_HEADS // 2
MIX_W = N_HEADS * HEAD_DIM
MLA_Q_RANK = 768
MLA_KV_RANK = 256
MLA_NOPE = 64
MLA_ROPE = 32
MLA_QK = MLA_NOPE + MLA_ROPE
N_MEM_HEADS = 4
MEM_Q = N_MEM_HEADS * HEAD_DIM
D_FF = 2816
ROPE_THETA = 10000.0
EPS = 1e-6
LOG2_E = 1.4426950408889634
SOFTMAX_C = MLA_QK ** -0.5 * LOG2_E
SUM_ROWS = 16
MEM_ROWS = 256
LANES = 128

VMEM_LIMIT = 56 * 1024 * 1024


def _cparams(*sem):
    return pltpu.CompilerParams(dimension_semantics=sem, vmem_limit_bytes=VMEM_LIMIT)


def _rms_rows(x, g):
    ms = jnp.mean(x * x, axis=-1, keepdims=True)
    return x * lax.rsqrt(ms + EPS) * g


def _rms_matmul_kernel(x_ref, g_ref, w_ref, *o_refs, segments, chunk):
    h = _rms_rows(x_ref[...].astype(F32), g_ref[...]).astype(BF16)
    for o_ref, (start, width) in zip(o_refs, segments):
        for c in range(0, width, chunk):
            cw = min(chunk, width - c)
            acc = jnp.dot(h, w_ref[:, start + c:start + c + cw], preferred_element_type=F32)
            o_ref[:, c:c + cw] = acc.astype(o_ref.dtype)


def _rms_matmul(x, g, w, segments, dtypes, tm, name):
    t, k = x.shape
    n = w.shape[1]
    kern = functools.partial(_rms_matmul_kernel, segments=tuple(segments), chunk=512)
    return pl.pallas_call(
        kern,
        grid=(t // tm,),
        in_specs=[
            pl.BlockSpec((tm, k), lambda i: (i, 0)),
            pl.BlockSpec((1, k), lambda i: (0, 0)),
            pl.BlockSpec((k, n), lambda i: (0, 0)),
        ],
        out_specs=[pl.BlockSpec((tm, wd), lambda i: (i, 0)) for _, wd in segments],
        out_shape=[jax.ShapeDtypeStruct((t, wd), dt) for (_, wd), dt in zip(segments, dtypes)],
        compiler_params=_cparams("parallel"),
        name=name,
    )(x, g.reshape(1, k), w)


N_SPLIT = 2
EXP_UNDERFLOW = -104.0
U_EXTRA = 16


def _sb_in_proj_kernel(x_ref, g_ref, w_ref, wvt_ref, qk_ref, qm_ref, vt_ref):
    h = _rms_rows(x_ref[...], g_ref[...]).astype(BF16)
    n_qk = qk_ref.shape[1]
    for c in range(0, n_qk, 512):
        qk_ref[:, c:c + 512] = jnp.dot(h, w_ref[:, c:c + 512],
                                       preferred_element_type=F32).astype(qk_ref.dtype)
    qm_ref[...] = jnp.dot(h, w_ref[:, n_qk:], preferred_element_type=F32)
    vt = lax.dot_general(wvt_ref[...], h, (((1,), (1,)), ((), ())), preferred_element_type=F32)
    tk = vt_ref.shape[-1]
    for c in range(vt_ref.shape[0]):
        vt_ref[c] = vt[:, c * tk:(c + 1) * tk].astype(vt_ref.dtype)


def _sb_in_proj(x, g, w_in, b, s, tm, tk):
    t = x.shape[0]
    w_qk_qm = jnp.concatenate([w_in[:, :2 * MIX_W], w_in[:, 3 * MIX_W:]], axis=1).astype(BF16)
    wvt = w_in[:, 2 * MIX_W:3 * MIX_W].T.astype(BF16)
    per_seq = s // tm
    row = lambda i: (i, 0)
    fixed = lambda i: (0, 0)
    return pl.pallas_call(
        _sb_in_proj_kernel,
        grid=(t // tm,),
        in_specs=[
            pl.BlockSpec((tm, D_MODEL), row),
            pl.BlockSpec((1, D_MODEL), fixed),
            pl.BlockSpec((D_MODEL, 2 * MIX_W + MEM_Q), fixed),
            pl.BlockSpec((MIX_W, D_MODEL), fixed),
        ],
        out_specs=[pl.BlockSpec((tm, 2 * MIX_W), row), pl.BlockSpec((tm, MEM_Q), row),
                   pl.BlockSpec((None, tm // tk, MIX_W, tk),
                                lambda i: (i // per_seq, i % per_seq, 0, 0))],
        out_shape=[jax.ShapeDtypeStruct((t, 2 * MIX_W), BF16),
                   jax.ShapeDtypeStruct((t, MEM_Q), F32),
                   jax.ShapeDtypeStruct((b, s // tk, MIX_W, tk), BF16)],
        compiler_params=_cparams("parallel"),
        name="sb_in_proj",
    )(x, g.reshape(1, D_MODEL), w_qk_qm, wvt)


def _split_rows(x):
    parts = []
    r = x
    for i in range(N_SPLIT):
        piece = r.astype(BF16)
        parts.append(piece)
        if i + 1 < N_SPLIT:
            r = r - piece.astype(F32)
    return jnp.concatenate(parts, axis=0)


def _sb_attn_kernel(q_ref, k_ref, vt_ref, u_ref, o_ref, *, tq, n_pairs):
    qi = pl.program_id(2)
    lane = lax.broadcasted_iota(jnp.int32, (tq, LANES), 1)
    key = lax.broadcasted_iota(jnp.int32, (tq, tq), 0)
    qry = lax.broadcasted_iota(jnp.int32, (tq, tq), 1)
    strict = key < qry
    u = u_ref[...]
    heads = [(p, h) for p in range(n_pairs) for h in range(2)]
    n = len(heads)
    qs = []
    for p, h in heads:
        in_head = (lane >= h * HEAD_DIM) & (lane < (h + 1) * HEAD_DIM)
        q_pair = q_ref[:, p * LANES:(p + 1) * LANES]
        qs.append(jnp.where(in_head, q_pair * (HEAD_DIM ** -0.5), 0).astype(BF16))

    def score_stage(kb, diagonal):
        start = pl.multiple_of(kb * tq, tq)
        zs = []
        for idx in range(n):
            p = heads[idx][0]
            k = k_ref[pl.ds(start, tq), p * LANES:(p + 1) * LANES]
            zs.append(lax.dot_general(k, qs[idx], (((1,), (1,)), ((), ())),
                                      preferred_element_type=F32))
        log_betas, log_fails = [], []
        for z in zs:
            soft = jnp.log(1.0 + jnp.exp(-jnp.abs(z)))
            log_beta = jnp.minimum(z, 0.0) - soft
            log_fail = log_beta - z
            if diagonal:
                log_fail = jnp.where(strict, log_fail, 0.0)
            log_betas.append(log_beta)
            log_fails.append(log_fail)
        return log_betas, log_fails

    def sum_stage(log_fails):
        return [jnp.dot(u, _split_rows(lf), preferred_element_type=F32) for lf in log_fails]

    def value_stage(kb, log_betas, sums, state, diagonal):
        ws = []
        for idx in range(n):
            w = jnp.exp(log_betas[idx] + sums[idx][:tq] + state[idx][0])
            if diagonal:
                w = jnp.where(strict, w, 0.0)
            ws.append(w.astype(BF16))
        new = []
        for idx in range(n):
            vt = vt_ref[kb, idx * HEAD_DIM:(idx + 1) * HEAD_DIM, :]
            acc = state[idx][1] + jnp.dot(vt, ws[idx], preferred_element_type=F32)
            carry = state[idx][0] + sums[idx][tq:tq + 1]
            new.append((carry, acc))
        return tuple(new)

    def block(kb, state):
        log_betas, log_fails = score_stage(kb, False)
        return value_stage(kb, log_betas, sum_stage(log_fails), state, False)

    zero = ((jnp.zeros((1, tq), F32), jnp.zeros((HEAD_DIM, tq), F32)),) * n

    def diagonal_only():
        log_betas, log_fails = score_stage(qi, True)
        return value_stage(qi, log_betas, sum_stage(log_fails), zero, True)

    def diagonal_and_previous():
        lb_d, lf_d = score_stage(qi, True)
        lb_p, lf_p = score_stage(qi - 1, False)
        sums_d = sum_stage(lf_d)
        sums_p = sum_stage(lf_p)
        state = value_stage(qi, lb_d, sums_d, zero, True)
        return value_stage(qi - 1, lb_p, sums_p, state, False)

    def live(state):
        top = state[0][0]
        for carry, _ in state[1:]:
            top = jnp.maximum(top, carry)
        return jnp.max(top) >= EXP_UNDERFLOW

    state = lax.cond(qi > 0, diagonal_and_previous, diagonal_only)

    def step(loop):
        i, _, state = loop
        state = block(qi - 1 - i, state)
        return i + 1, live(state), state

    _, _, state = lax.while_loop(lambda loop: (loop[0] < qi) & loop[1], step,
                                 (jnp.int32(1), live(state), state))
    for p in range(n_pairs):
        pair = jnp.concatenate([state[2 * p][1], state[2 * p + 1][1]], axis=0)
        o_ref[:, p * LANES:(p + 1) * LANES] = pair.T.astype(o_ref.dtype)


def _sb_attention(qk, vt, b, s, tq, n_pairs):
    r = jnp.arange(tq)
    u = (r[None, :] > r[:, None]).astype(BF16)
    u = jnp.concatenate([u, jnp.ones((U_EXTRA, tq), BF16)], axis=0)
    u = jnp.concatenate([u] * N_SPLIT, axis=1)
    kern = functools.partial(_sb_attn_kernel, tq=tq, n_pairs=n_pairs)
    w = n_pairs * LANES
    groups = N_PAIRS // n_pairs
    return pl.pallas_call(
        kern,
        grid=(b, groups, s // tq),
        in_specs=[
            pl.BlockSpec((None, tq, w), lambda bi, g, qi: (bi, qi, g)),
            pl.BlockSpec((None, s, w), lambda bi, g, qi: (bi, 0, groups + g)),
            pl.BlockSpec((None, s // tq, w, tq), lambda bi, g, qi: (bi, 0, g, 0)),
            pl.BlockSpec((tq + U_EXTRA, N_SPLIT * tq), lambda bi, g, qi: (0, 0)),
        ],
        out_specs=pl.BlockSpec((None, tq, w), lambda bi, g, qi: (bi, qi, g)),
        out_shape=jax.ShapeDtypeStruct((b, s, MIX_W), BF16),
        compiler_params=_cparams("parallel", "parallel", "arbitrary"),
        name="sb_attention",
    )(qk, qk, vt, u)


def _rope_table_kernel(pos_ref, invf_ref, sign_ref, cos_ref, sin_ref):
    ang = pos_ref[...].astype(F32) * invf_ref[...]
    cos_ref[...] = jnp.cos(ang)
    sin_ref[...] = jnp.sin(ang) * sign_ref[...]


ROPE_HALF = MLA_ROPE // 2
ROT_LO = MLA_NOPE + ROPE_HALF


def _rope_tables(positions, tm):
    t = positions.size
    inv_freq = ROPE_THETA ** (-jnp.arange(ROPE_HALF, dtype=F32) / ROPE_HALF)
    zeros = jnp.zeros((MLA_NOPE,), F32)
    ones = jnp.ones((ROPE_HALF,), F32)
    invf = jnp.concatenate([zeros] + [inv_freq] * 4)
    sign = jnp.concatenate([zeros, -ones, ones, -ones, ones])
    fixed = pl.BlockSpec((1, LANES), lambda i: (0, 0))
    return pl.pallas_call(
        _rope_table_kernel,
        grid=(t // tm,),
        in_specs=[pl.BlockSpec((tm, 1), lambda i: (i, 0)), fixed, fixed],
        out_specs=[pl.BlockSpec((tm, LANES), lambda i: (i, 0))] * 2,
        out_shape=[jax.ShapeDtypeStruct((t, LANES), F32)] * 2,
        compiler_params=_cparams("parallel"),
        name="rope_tables",
    )(positions.reshape(t, 1), invf.reshape(1, LANES), sign.reshape(1, LANES))


def _mla_q_kernel(cq_ref, gq_ref, w_ref, gh_ref, real_ref, cos_ref, sin_ref, q_ref):
    h = _rms_rows(cq_ref[...], gq_ref[...]).astype(BF16)
    real = real_ref[...]
    lane = lax.broadcasted_iota(jnp.int32, cos_ref.shape, 1)
    gain_table = gh_ref[...] * jnp.where(lane < MLA_QK, cos_ref[...], sin_ref[...])
    for hd in range(N_HEADS):
        sl = slice(hd * LANES, (hd + 1) * LANES)
        x = jnp.dot(h, w_ref[:, sl], preferred_element_type=F32)
        ms = jnp.sum(x * x * real, axis=-1, keepdims=True) / MLA_QK
        q_ref[:, sl] = (x * lax.rsqrt(ms + EPS) * gain_table).astype(q_ref.dtype)


def _mla_k_kernel(ckv_ref, gkv_ref, w_ref, wvt_ref, kr_ref, gh_ref, real_ref, cos_ref, sin_ref,
                  k_ref, vt_ref):
    h = _rms_rows(ckv_ref[...], gkv_ref[...]).astype(BF16)
    cosf = cos_ref[...]
    sinf = sin_ref[...]
    g = gh_ref[...]
    real = real_ref[...]
    k_rope = kr_ref[...]
    rope_sq = jnp.sum(k_rope * k_rope * real, axis=-1, keepdims=True)
    k_rot = k_rope * g
    lane = lax.broadcasted_iota(jnp.int32, k_rot.shape, 1)
    partner = jnp.where(lane < ROT_LO, pltpu.roll(k_rot, LANES - ROPE_HALF, 1),
                        pltpu.roll(k_rot, ROPE_HALF, 1))
    k_rot = k_rot * cosf + partner * sinf
    for hd in range(N_HEADS):
        sl = slice(hd * LANES, (hd + 1) * LANES)
        x = jnp.dot(h, w_ref[:, sl], preferred_element_type=F32)
        ms = (jnp.sum(x * x, axis=-1, keepdims=True) + rope_sq) / MLA_QK
        k_ref[:, sl] = ((x * g + k_rot) * lax.rsqrt(ms + EPS)).astype(k_ref.dtype)
    vt = lax.dot_general(wvt_ref[...], h, (((1,), (1,)), ((), ())), preferred_element_type=F32)
    tk = vt_ref.shape[-1]
    for c in range(vt_ref.shape[0]):
        vt_ref[c] = vt[:, c * tk:(c + 1) * tk].astype(vt_ref.dtype)


def _key_head_gain(g):
    g = g.astype(F32)
    return jnp.concatenate([g, g[MLA_NOPE:]]).reshape(1, LANES)


def _query_head_gain(g):
    g = g.astype(F32)
    return (jnp.concatenate([g, g[ROT_LO:], g[MLA_NOPE:ROT_LO]]) * SOFTMAX_C).reshape(1, LANES)


def _mla_prep(c_q, c_kv, k_rope, cosf, sinf, q_norm, w_q_up, kv_norm, w_kv_up, g_qn, g_kn,
              b, s, tm, tk):
    t = c_q.shape[0]
    wq = w_q_up.reshape(MLA_Q_RANK, N_HEADS, MLA_QK)
    wq = jnp.concatenate([wq, wq[:, :, ROT_LO:], wq[:, :, MLA_NOPE:ROT_LO]], axis=2)
    wq = wq.reshape(MLA_Q_RANK, N_HEADS * LANES).astype(BF16)
    real = (jnp.arange(LANES) < MLA_QK).astype(F32).reshape(1, LANES)
    wkv = w_kv_up.reshape(MLA_KV_RANK, N_HEADS, MLA_NOPE + HEAD_DIM)
    wk = jnp.pad(wkv[:, :, :MLA_NOPE], ((0, 0), (0, 0), (0, LANES - MLA_NOPE)))
    wk = wk.reshape(MLA_KV_RANK, N_HEADS * LANES).astype(BF16)
    wvt = wkv[:, :, MLA_NOPE:].reshape(MLA_KV_RANK, MIX_W).T.astype(BF16)

    row = lambda i: (i, 0)
    fixed = lambda i: (0, 0)
    q = pl.pallas_call(
        _mla_q_kernel,
        grid=(t // tm,),
        in_specs=[
            pl.BlockSpec((tm, MLA_Q_RANK), row),
            pl.BlockSpec((1, MLA_Q_RANK), fixed),
            pl.BlockSpec((MLA_Q_RANK, N_HEADS * LANES), fixed),
            pl.BlockSpec((1, LANES), fixed),
            pl.BlockSpec((1, LANES), fixed),
            pl.BlockSpec((tm, LANES), row),
            pl.BlockSpec((tm, LANES), row),
        ],
        out_specs=pl.BlockSpec((tm, N_HEADS * LANES), row),
        out_shape=jax.ShapeDtypeStruct((t, N_HEADS * LANES), BF16),
        compiler_params=_cparams("parallel"),
        name="mla_q_prep",
    )(c_q, q_norm.reshape(1, -1), wq, _query_head_gain(g_qn), real, cosf, sinf)
    per_seq = s // tm
    k, vt = pl.pallas_call(
        _mla_k_kernel,
        grid=(t // tm,),
        in_specs=[
            pl.BlockSpec((tm, MLA_KV_RANK), row),
            pl.BlockSpec((1, MLA_KV_RANK), fixed),
            pl.BlockSpec((MLA_KV_RANK, N_HEADS * LANES), fixed),
            pl.BlockSpec((MIX_W, MLA_KV_RANK), fixed),
            pl.BlockSpec((tm, LANES), row),
            pl.BlockSpec((1, LANES), fixed),
            pl.BlockSpec((1, LANES), fixed),
            pl.BlockSpec((tm, LANES), row),
            pl.BlockSpec((tm, LANES), row),
        ],
        out_specs=[pl.BlockSpec((tm, N_HEADS * LANES), row),
                   pl.BlockSpec((None, tm // tk, MIX_W, tk),
                                lambda i: (i // per_seq, i % per_seq, 0, 0))],
        out_shape=[jax.ShapeDtypeStruct((t, N_HEADS * LANES), BF16),
                   jax.ShapeDtypeStruct((b, s // tk, MIX_W, tk), BF16)],
        compiler_params=_cparams("parallel"),
        name="mla_kv_prep",
    )(c_kv, kv_norm.reshape(1, -1), wk, wvt, k_rope, _key_head_gain(g_kn), real, cosf, sinf)
    return q, k, vt


def _causal_attn_kernel(q_ref, k_ref, vt_ref, o_ref, sa_ref, sb_ref, *, tq, n_pairs):
    qi = pl.program_id(2)
    key = lax.broadcasted_iota(jnp.int32, (tq, tq), 0)
    qry = lax.broadcasted_iota(jnp.int32, (tq, tq), 1)
    causal = key <= qry
    n_heads = 2 * n_pairs
    qs = [q_ref[:, h * LANES:(h + 1) * LANES] for h in range(n_heads)]

    def score_tiles(kb):
        start = pl.multiple_of(kb * tq, tq)
        return [lax.dot_general(k_ref[pl.ds(start, tq), h * LANES:(h + 1) * LANES], qs[h],
                                (((1,), (1,)), ((), ())), preferred_element_type=F32)
                for h in range(n_heads)]

    ones = jnp.ones((SUM_ROWS, tq), BF16)

    def park(buf, scs, ms, diagonal):
        new_ms, alphas = [], []
        for h in range(n_heads):
            sc = jnp.where(causal, scs[h], -jnp.inf) if diagonal else scs[h]
            m_new = jnp.maximum(ms[h], jnp.max(sc, axis=0, keepdims=True))
            alphas.append(jnp.exp2(ms[h] - m_new))
            new_ms.append(m_new)
            buf[h] = sc
        return tuple(new_ms), tuple(alphas)

    def consume(buf, kb, ms, alphas, accs):
        new_accs = []
        for h in range(n_heads):
            p = jnp.exp2(buf[h] - ms[h]).astype(BF16)
            vt = jnp.concatenate([vt_ref[kb, h * HEAD_DIM:(h + 1) * HEAD_DIM, :], ones], axis=0)
            new_accs.append(alphas[h] * accs[h] + jnp.dot(vt, p, preferred_element_type=F32))
        return tuple(new_accs)

    row = lambda v: (jnp.full((1, tq), v, F32),) * n_heads
    ms, alphas = park(sb_ref, score_tiles(qi), row(-jnp.inf), True)
    accs = (jnp.zeros((HEAD_DIM + SUM_ROWS, tq), F32),) * n_heads

    def double(t, carry):
        ms, alphas, accs = carry
        j = 2 * t
        scs_a = score_tiles(j)
        scs_b = score_tiles(j + 1)
        ms_a, alphas_a = park(sa_ref, scs_a, ms, False)
        accs = consume(sb_ref, jnp.where(j == 0, qi, j - 1), ms, alphas, accs)
        accs = consume(sa_ref, j, ms_a, alphas_a, accs)
        ms, alphas = park(sb_ref, scs_b, ms_a, False)
        return ms, alphas, accs

    ms, alphas, accs = lax.fori_loop(0, qi // 2, double, (ms, alphas, accs))
    before_last = jnp.where(qi <= 1, qi, qi - 2)

    def odd_tail(ms, alphas, accs):
        ms_a, alphas_a = park(sa_ref, score_tiles(qi - 1), ms, False)
        accs = consume(sb_ref, before_last, ms, alphas, accs)
        return consume(sa_ref, qi - 1, ms_a, alphas_a, accs)

    def even_tail(ms, alphas, accs):
        return consume(sb_ref, jnp.where(qi == 0, qi, qi - 1), ms, alphas, accs)

    accs = lax.cond(qi % 2 == 1, odd_tail, even_tail, ms, alphas, accs)
    outs = [acc[:HEAD_DIM] / acc[HEAD_DIM:HEAD_DIM + 1] for acc in accs]
    for p in range(n_pairs):
        pair = jnp.concatenate([outs[2 * p], outs[2 * p + 1]], axis=0)
        o_ref[:, p * LANES:(p + 1) * LANES] = pair.T.astype(o_ref.dtype)


def _causal_attention(q, k, vt, b, s, tq, n_pairs):
    kern = functools.partial(_causal_attn_kernel, tq=tq, n_pairs=n_pairs)
    wq = 2 * n_pairs * LANES
    wv = n_pairs * LANES
    return pl.pallas_call(
        kern,
        grid=(b, N_PAIRS // n_pairs, s // tq),
        in_specs=[
            pl.BlockSpec((None, tq, wq), lambda bi, g, qi: (bi, qi, g)),
            pl.BlockSpec((None, s, wq), lambda bi, g, qi: (bi, 0, g)),
            pl.BlockSpec((None, s // tq, wv, tq), lambda bi, g, qi: (bi, 0, g, 0)),
        ],
        out_specs=pl.BlockSpec((None, tq, wv), lambda bi, g, qi: (bi, qi, g)),
        out_shape=jax.ShapeDtypeStruct((b, s, MIX_W), BF16),
        scratch_shapes=[pltpu.VMEM((2 * n_pairs, tq, tq), F32)] * 2,
        compiler_params=_cparams("parallel", "parallel", "arbitrary"),
        name="causal_attention",
    )(q, k, vt)


def _mem_kv_kernel(kv_ref, g_ref, k_ref, v_ref):
    k = kv_ref[:, :MEM_Q]
    v = kv_ref[:, MEM_Q:]
    g = g_ref[...]
    lane = lax.broadcasted_iota(jnp.int32, k.shape, 1)
    for h in range(N_MEM_HEADS):
        in_head = (lane >= h * HEAD_DIM) & (lane < (h + 1) * HEAD_DIM)
        ms = jnp.sum(jnp.where(in_head, k * k, 0.0), axis=-1, keepdims=True) / HEAD_DIM
        k_ref[h] = jnp.where(in_head, k * lax.rsqrt(ms + EPS) * g, 0.0).astype(k_ref.dtype)
        v_ref[h] = jnp.where(in_head, v, 0.0).astype(v_ref.dtype)


def _mem_out_kernel(mix_ref, qm_ref, k_ref, v_ref, g_ref, w_ref, x_ref, o_ref):
    tm = qm_ref.shape[0]
    rows = [slice(r, r + MEM_ROWS) for r in range(0, tm, MEM_ROWS)]
    outs = [jnp.dot(mix_ref[r, :], w_ref[:MIX_W, :], preferred_element_type=F32) for r in rows]
    lane = lax.broadcasted_iota(jnp.int32, (MEM_ROWS, MEM_Q), 1)
    qns = []
    for r in rows:
        q = qm_ref[r, :]
        inv = jnp.zeros_like(q)
        for h in range(N_MEM_HEADS):
            in_head = (lane >= h * HEAD_DIM) & (lane < (h + 1) * HEAD_DIM)
            ms = jnp.sum(jnp.where(in_head, q * q, 0.0), axis=-1, keepdims=True) / HEAD_DIM
            inv = jnp.where(in_head, lax.rsqrt(ms + EPS), inv)
        qns.append((q * inv * g_ref[...]).astype(BF16))
    scs = [[lax.dot_general(qn, k_ref[h], (((1,), (1,)), ((), ())), preferred_element_type=F32)
            for h in range(N_MEM_HEADS)] for qn in qns]
    ps = []
    for chunk in scs:
        ps.append([])
        for sc in chunk:
            sc = sc * (HEAD_DIM ** -0.5)
            e = jnp.exp(sc - jnp.max(sc, axis=-1, keepdims=True))
            ps[-1].append((e / jnp.sum(e, axis=-1, keepdims=True)).astype(BF16))
    for r, out, p in zip(rows, outs, ps):
        mem_o = jnp.dot(p[0], v_ref[0], preferred_element_type=F32)
        for h in range(1, N_MEM_HEADS):
            mem_o = mem_o + jnp.dot(p[h], v_ref[h], preferred_element_type=F32)
        out = out + jnp.dot(mem_o.astype(BF16), w_ref[MIX_W:, :], preferred_element_type=F32)
        o_ref[r, :] = x_ref[r, :] + out


def _mem_kv_prep(mem2d, ln_g, w_kv, g_kn, b, n_mem):
    w = w_kv.reshape(D_MODEL, N_MEM_HEADS, 2, HEAD_DIM).transpose(0, 2, 1, 3)
    w = w.reshape(D_MODEL, 2 * MEM_Q).astype(BF16)
    (kv,) = _rms_matmul(mem2d, ln_g, w, [(0, 2 * MEM_Q)], [F32], tm=n_mem, name="mem_kv_proj")
    shape = jax.ShapeDtypeStruct((b, N_MEM_HEADS, n_mem, MEM_Q), BF16)
    spec = pl.BlockSpec((None, N_MEM_HEADS, n_mem, MEM_Q), lambda i: (i, 0, 0, 0))
    return pl.pallas_call(
        _mem_kv_kernel,
        grid=(b,),
        in_specs=[pl.BlockSpec((n_mem, 2 * MEM_Q), lambda i: (i, 0)),
                  pl.BlockSpec((1, MEM_Q), lambda i: (0, 0))],
        out_specs=[spec, spec],
        out_shape=[shape, shape],
        compiler_params=_cparams("parallel"),
        name="mem_kv_prep",
    )(kv, jnp.tile(g_kn.astype(F32), N_MEM_HEADS).reshape(1, MEM_Q))


def _mem_out_proj(mix, q_mem, k_mem, v_mem, g_qn, w_out, x, s, tm):
    t = x.shape[0]
    n_mem = k_mem.shape[2]
    per_seq = s // tm
    row = lambda i: (i, 0)
    mem_spec = pl.BlockSpec((None, N_MEM_HEADS, n_mem, MEM_Q), lambda i: (i // per_seq, 0, 0, 0))
    return pl.pallas_call(
        _mem_out_kernel,
        grid=(t // tm,),
        in_specs=[
            pl.BlockSpec((tm, MIX_W), row),
            pl.BlockSpec((tm, MEM_Q), row),
            mem_spec,
            mem_spec,
            pl.BlockSpec((1, MEM_Q), lambda i: (0, 0)),
            pl.BlockSpec((D_MODEL, D_MODEL), lambda i: (0, 0)),
            pl.BlockSpec((tm, D_MODEL), row),
        ],
        out_specs=pl.BlockSpec((tm, D_MODEL), row),
        out_shape=jax.ShapeDtypeStruct((t, D_MODEL), F32),
        compiler_params=_cparams("parallel"),
        name="mem_attn_out_proj",
    )(mix, q_mem, k_mem, v_mem, jnp.tile(g_qn.astype(F32), N_MEM_HEADS).reshape(1, MEM_Q),
      w_out, x)


def _ffn_kernel(x_ref, g_ref, wgu_ref, wd_ref, o_ref, *, tf):
    x = x_ref[...]
    h = _rms_rows(x, g_ref[...]).astype(BF16)
    acc = x
    for c in range(0, D_FF, tf):
        gate = jnp.dot(h, wgu_ref[:, c:c + tf], preferred_element_type=F32)
        up = jnp.dot(h, wgu_ref[:, D_FF + c:D_FF + c + tf], preferred_element_type=F32)
        act = gate * (1.0 / (1.0 + jnp.exp(-gate))) * up
        acc = acc + jnp.dot(act.astype(BF16), wd_ref[c:c + tf, :], preferred_element_type=F32)
    o_ref[...] = acc


def _ffn(x, g, w_gu, w_down, tm, tf):
    t = x.shape[0]
    once = pl.Buffered(1)
    return pl.pallas_call(
        functools.partial(_ffn_kernel, tf=tf),
        grid=(t // tm,),
        in_specs=[
            pl.BlockSpec((tm, D_MODEL), lambda i: (i, 0)),
            pl.BlockSpec((1, D_MODEL), lambda i: (0, 0)),
            pl.BlockSpec((D_MODEL, 2 * D_FF), lambda i: (0, 0), pipeline_mode=once),
            pl.BlockSpec((D_FF, D_MODEL), lambda i: (0, 0), pipeline_mode=once),
        ],
        out_specs=pl.BlockSpec((tm, D_MODEL), lambda i: (i, 0)),
        out_shape=jax.ShapeDtypeStruct((t, D_MODEL), F32),
        compiler_params=_cparams("parallel"),
        name="swiglu_ffn",
    )(x, g.reshape(1, D_MODEL), w_gu, w_down)


def kernel(x, mem, positions, ln_attn, w_out, ln_mem, w_mem_kv, g_qn_mem, g_kn_mem, ln_ffn,
           w_ffn_gu, w_ffn_down, sb_w_in, mla_w_in, mla_q_norm, mla_w_q_up, mla_kv_norm,
           mla_w_kv_up, mla_g_qn, mla_g_kn):
    b, s, d = x.shape
    n_mem = mem.shape[1]
    t = b * s
    x2 = x.reshape(t, d)
    mem2 = mem.reshape(b * n_mem, d)

    qk, q_mem, vt = _sb_in_proj(x2, ln_attn[0], sb_w_in[0], b, s, tm=512, tk=256)
    mix = _sb_attention(qk.reshape(b, s, 2 * MIX_W), vt, b, s, tq=256, n_pairs=2).reshape(t, MIX_W)
    k_mem, v_mem = _mem_kv_prep(mem2, ln_mem[0], w_mem_kv[0], g_kn_mem[0], b, n_mem)
    x2 = _mem_out_proj(mix, q_mem, k_mem, v_mem, g_qn_mem[0], w_out[0].astype(BF16), x2, s, tm=1024)
    x2 = _ffn(x2, ln_ffn[0], w_ffn_gu[0].astype(BF16), w_ffn_down[0].astype(BF16), tm=1024, tf=256)

    w_in = mla_w_in[0]
    o_kr = MLA_Q_RANK + MLA_KV_RANK
    o_qm = o_kr + MLA_ROPE
    w_kr = jnp.concatenate([jnp.zeros((D_MODEL, MLA_NOPE), F32), w_in[:, o_kr:o_qm],
                            w_in[:, o_kr:o_qm]], axis=1)
    w_in = jnp.concatenate([w_in[:, :o_kr], w_in[:, o_qm:], w_kr], axis=1).astype(BF16)
    c_q, c_kv, q_mem, k_rope = _rms_matmul(
        x2, ln_attn[1], w_in,
        [(0, MLA_Q_RANK), (MLA_Q_RANK, MLA_KV_RANK), (o_kr, MEM_Q), (o_kr + MEM_Q, LANES)],
        [F32, F32, F32, F32], tm=512, name="mla_in_proj")
    cosf, sinf = _rope_tables(positions, tm=512)
    q, k, vt = _mla_prep(c_q, c_kv, k_rope, cosf, sinf, mla_q_norm[0], mla_w_q_up[0],
                         mla_kv_norm[0], mla_w_kv_up[0], mla_g_qn[0], mla_g_kn[0], b, s,
                         tm=512, tk=256)
    mix = _causal_attention(q.reshape(b, s, -1), k.reshape(b, s, -1), vt,
                            b, s, tq=256, n_pairs=2).reshape(t, MIX_W)
    k_mem, v_mem = _mem_kv_prep(mem2, ln_mem[1], w_mem_kv[1], g_kn_mem[1], b, n_mem)
    x2 = _mem_out_proj(mix, q_mem, k_mem, v_mem, g_qn_mem[1], w_out[1].astype(BF16), x2, s, tm=1024)
    x2 = _ffn(x2, ln_ffn[1], w_ffn_gu[1].astype(BF16), w_ffn_down[1].astype(BF16), tm=1024, tf=256)
    return x2.reshape(b, s, d)
```

```python
import functools

import jax
import jax.numpy as jnp
from jax import lax
from jax.experimental import pallas as pl
from jax.experimental.pallas import tpu as pltpu

F32 = jnp.float32
BF16 = jnp.bfloat16

D_MODEL = 1024
HEAD_DIM = 64
N_HEADS = 12
N_PAIRS = N_HEADS // 2
MIX_W = N_HEADS * HEAD_DIM
MLA_Q_RANK = 768
MLA_KV_RANK = 256
MLA_NOPE = 64
MLA_ROPE = 32
MLA_QK = MLA_NOPE + MLA_ROPE
N_MEM_HEADS = 4
MEM_Q = N_MEM_HEADS * HEAD_DIM
D_FF = 2816
ROPE_THETA = 10000.0
EPS = 1e-6
LOG2_E = 1.4426950408889634
SOFTMAX_C = MLA_QK ** -0.5 * LOG2_E
SUM_ROWS = 16
MEM_ROWS = 256
LANES = 128

VMEM_LIMIT = 56 * 1024 * 1024


def _cparams(*sem):
    return pltpu.CompilerParams(dimension_semantics=sem, vmem_limit_bytes=VMEM_LIMIT)


def _rms_rows(x, g):
    ms = jnp.mean(x * x, axis=-1, keepdims=True)
    return x * lax.rsqrt(ms + EPS) * g


def _rms_matmul_kernel(x_ref, g_ref, w_ref, *o_refs, segments, chunk):
    h = _rms_rows(x_ref[...].astype(F32), g_ref[...]).astype(BF16)
    for o_ref, (start, width) in zip(o_refs, segments):
        for c in range(0, width, chunk):
            cw = min(chunk, width - c)
            acc = jnp.dot(h, w_ref[:, start + c:start + c + cw], preferred_element_type=F32)
            o_ref[:, c:c + cw] = acc.astype(o_ref.dtype)


def _rms_matmul(x, g, w, segments, dtypes, tm, name):
    t, k = x.shape
    n = w.shape[1]
    kern = functools.partial(_rms_matmul_kernel, segments=tuple(segments), chunk=512)
    return pl.pallas_call(
        kern,
        grid=(t // tm,),
        in_specs=[
            pl.BlockSpec((tm, k), lambda i: (i, 0)),
            pl.BlockSpec((1, k), lambda i: (0, 0)),
            pl.BlockSpec((k, n), lambda i: (0, 0)),
        ],
        out_specs=[pl.BlockSpec((tm, wd), lambda i: (i, 0)) for _, wd in segments],
        out_shape=[jax.ShapeDtypeStruct((t, wd), dt) for (_, wd), dt in zip(segments, dtypes)],
        compiler_params=_cparams("parallel"),
        name=name,
    )(x, g.reshape(1, k), w)


N_SPLIT = 2
EXP_UNDERFLOW = -104.0
U_EXTRA = 16


def _sb_in_proj_kernel(x_ref, g_ref, w_ref, wvt_ref, qk_ref, qm_ref, vt_ref):
    h = _rms_rows(x_ref[...], g_ref[...]).astype(BF16)
    n_qk = qk_ref.shape[1]
    for c in range(0, n_qk, 512):
        qk_ref[:, c:c + 512] = jnp.dot(h, w_ref[:, c:c + 512],
                                       preferred_element_type=F32).astype(qk_ref.dtype)
    qm_ref[...] = jnp.dot(h, w_ref[:, n_qk:], preferred_element_type=F32)
    vt = lax.dot_general(wvt_ref[...], h, (((1,), (1,)), ((), ())), preferred_element_type=F32)
    tk = vt_ref.shape[-1]
    for c in range(vt_ref.shape[0]):
        vt_ref[c] = vt[:, c * tk:(c + 1) * tk].astype(vt_ref.dtype)


def _sb_in_proj(x, g, w_in, b, s, tm, tk):
    t = x.shape[0]
    w_qk_qm = jnp.concatenate([w_in[:, :2 * MIX_W], w_in[:, 3 * MIX_W:]], axis=1).astype(BF16)
    wvt = w_in[:, 2 * MIX_W:3 * MIX_W].T.astype(BF16)
    per_seq = s // tm
    row = lambda i: (i, 0)
    fixed = lambda i: (0, 0)
    return pl.pallas_call(
        _sb_in_proj_kernel,
        grid=(t // tm,),
        in_specs=[
            pl.BlockSpec((tm, D_MODEL), row),
            pl.BlockSpec((1, D_MODEL), fixed),
            pl.BlockSpec((D_MODEL, 2 * MIX_W + MEM_Q), fixed),
            pl.BlockSpec((MIX_W, D_MODEL), fixed),
        ],
        out_specs=[pl.BlockSpec((tm, 2 * MIX_W), row), pl.BlockSpec((tm, MEM_Q), row),
                   pl.BlockSpec((None, tm // tk, MIX_W, tk),
                                lambda i: (i // per_seq, i % per_seq, 0, 0))],
        out_shape=[jax.ShapeDtypeStruct((t, 2 * MIX_W), BF16),
                   jax.ShapeDtypeStruct((t, MEM_Q), F32),
                   jax.ShapeDtypeStruct((b, s // tk, MIX_W, tk), BF16)],
        compiler_params=_cparams("parallel"),
        name="sb_in_proj",
    )(x, g.reshape(1, D_MODEL), w_qk_qm, wvt)


def _split_rows(x):
    parts = []
    r = x
    for i in range(N_SPLIT):
        piece = r.astype(BF16)
        parts.append(piece)
        if i + 1 < N_SPLIT:
            r = r - piece.astype(F32)
    return jnp.concatenate(parts, axis=0)


def _sb_attn_kernel(q_ref, k_ref, vt_ref, u_ref, o_ref, *, tq, n_pairs):
    qi = pl.program_id(2)
    lane = lax.broadcasted_iota(jnp.int32, (tq, LANES), 1)
    key = lax.broadcasted_iota(jnp.int32, (tq, tq), 0)
    qry = lax.broadcasted_iota(jnp.int32, (tq, tq), 1)
    strict = key < qry
    u = u_ref[...]
    heads = [(p, h) for p in range(n_pairs) for h in range(2)]
    n = len(heads)
    qs = []
    for p, h in heads:
        in_head = (lane >= h * HEAD_DIM) & (lane < (h + 1) * HEAD_DIM)
        q_pair = q_ref[:, p * LANES:(p + 1) * LANES]
        qs.append(jnp.where(in_head, q_pair * (HEAD_DIM ** -0.5), 0).astype(BF16))

    def score_stage(kb, diagonal):
        start = pl.multiple_of(kb * tq, tq)
        zs = []
        for idx in range(n):
            p = heads[idx][0]
            k = k_ref[pl.ds(start, tq), p * LANES:(p + 1) * LANES]
            zs.append(lax.dot_general(k, qs[idx], (((1,), (1,)), ((), ())),
                                      preferred_element_type=F32))
        log_betas, log_fails = [], []
        for z in zs:
            soft = jnp.log(1.0 + jnp.exp(-jnp.abs(z)))
            log_beta = jnp.minimum(z, 0.0) - soft
            log_fail = log_beta - z
            if diagonal:
                log_fail = jnp.where(strict, log_fail, 0.0)
            log_betas.append(log_beta)
            log_fails.append(log_fail)
        return log_betas, log_fails

    def sum_stage(log_fails):
        return [jnp.dot(u, _split_rows(lf), preferred_element_type=F32) for lf in log_fails]

    def value_stage(kb, log_betas, sums, state, diagonal):
        ws = []
        for idx in range(n):
            w = jnp.exp(log_betas[idx] + sums[idx][:tq] + state[idx][0])
            if diagonal:
                w = jnp.where(strict, w, 0.0)
            ws.append(w.astype(BF16))
        new = []
        for idx in range(n):
            vt = vt_ref[kb, idx * HEAD_DIM:(idx + 1) * HEAD_DIM, :]
            acc = state[idx][1] + jnp.dot(vt, ws[idx], preferred_element_type=F32)
            carry = state[idx][0] + sums[idx][tq:tq + 1]
            new.append((carry, acc))
        return tuple(new)

    def block(kb, state):
        log_betas, log_fails = score_stage(kb, False)
        return value_stage(kb, log_betas, sum_stage(log_fails), state, False)

    zero = ((jnp.zeros((1, tq), F32), jnp.zeros((HEAD_DIM, tq), F32)),) * n

    def diagonal_only():
        log_betas, log_fails = score_stage(qi, True)
        return value_stage(qi, log_betas, sum_stage(log_fails), zero, True)

    def diagonal_and_previous():
        lb_d, lf_d = score_stage(qi, True)
        lb_p, lf_p = score_stage(qi - 1, False)
        sums_d = sum_stage(lf_d)
        sums_p = sum_stage(lf_p)
        state = value_stage(qi, lb_d, sums_d, zero, True)
        return value_stage(qi - 1, lb_p, sums_p, state, False)

    def live(state):
        top = state[0][0]
        for carry, _ in state[1:]:
            top = jnp.maximum(top, carry)
        return jnp.max(top) >= EXP_UNDERFLOW

    state = lax.cond(qi > 0, diagonal_and_previous, diagonal_only)

    def step(loop):
        i, _, state = loop
        state = block(qi - 1 - i, state)
        return i + 1, live(state), state

    _, _, state = lax.while_loop(lambda loop: (loop[0] < qi) & loop[1], step,
                                 (jnp.int32(1), live(state), state))
    for p in range(n_pairs):
        pair = jnp.concatenate([state[2 * p][1], state[2 * p + 1][1]], axis=0)
        o_ref[:, p * LANES:(p + 1) * LANES] = pair.T.astype(o_ref.dtype)


def _sb_attention(qk, vt, b, s, tq, n_pairs):
    r = jnp.arange(tq)
    u = (r[None, :] > r[:, None]).astype(BF16)
    u = jnp.concatenate([u, jnp.ones((U_EXTRA, tq), BF16)], axis=0)
    u = jnp.concatenate([u] * N_SPLIT, axis=1)
    kern = functools.partial(_sb_attn_kernel, tq=tq, n_pairs=n_pairs)
    w = n_pairs * LANES
    groups = N_PAIRS // n_pairs
    return pl.pallas_call(
        kern,
        grid=(b, groups, s // tq),
        in_specs=[
            pl.BlockSpec((None, tq, w), lambda bi, g, qi: (bi, qi, g)),
            pl.BlockSpec((None, s, w), lambda bi, g, qi: (bi, 0, groups + g)),
            pl.BlockSpec((None, s // tq, w, tq), lambda bi, g, qi: (bi, 0, g, 0)),
            pl.BlockSpec((tq + U_EXTRA, N_SPLIT * tq), lambda bi, g, qi: (0, 0)),
        ],
        out_specs=pl.BlockSpec((None, tq, w), lambda bi, g, qi: (bi, qi, g)),
        out_shape=jax.ShapeDtypeStruct((b, s, MIX_W), BF16),
        compiler_params=_cparams("parallel", "parallel", "arbitrary"),
        name="sb_attention",
    )(qk, qk, vt, u)


def _rope_table_kernel(pos_ref, invf_ref, sign_ref, cos_ref, sin_ref):
    ang = pos_ref[...].astype(F32) * invf_ref[...]
    cos_ref[...] = jnp.cos(ang)
    sin_ref[...] = jnp.sin(ang) * sign_ref[...]


ROPE_HALF = MLA_ROPE // 2
ROT_LO = MLA_NOPE + ROPE_HALF


def _rope_tables(positions, tm):
    t = positions.size
    inv_freq = ROPE_THETA ** (-jnp.arange(ROPE_HALF, dtype=F32) / ROPE_HALF)
    zeros = jnp.zeros((MLA_NOPE,), F32)
    ones = jnp.ones((ROPE_HALF,), F32)
    invf = jnp.concatenate([zeros] + [inv_freq] * 4)
    sign = jnp.concatenate([zeros, -ones, ones, -ones, ones])
    fixed = pl.BlockSpec((1, LANES), lambda i: (0, 0))
    return pl.pallas_call(
        _rope_table_kernel,
        grid=(t // tm,),
        in_specs=[pl.BlockSpec((tm, 1), lambda i: (i, 0)), fixed, fixed],
        out_specs=[pl.BlockSpec((tm, LANES), lambda i: (i, 0))] * 2,
        out_shape=[jax.ShapeDtypeStruct((t, LANES), F32)] * 2,
        compiler_params=_cparams("parallel"),
        name="rope_tables",
    )(positions.reshape(t, 1), invf.reshape(1, LANES), sign.reshape(1, LANES))


def _mla_q_kernel(cq_ref, gq_ref, w_ref, gh_ref, real_ref, cos_ref, sin_ref, q_ref):
    h = _rms_rows(cq_ref[...], gq_ref[...]).astype(BF16)
    real = real_ref[...]
    lane = lax.broadcasted_iota(jnp.int32, cos_ref.shape, 1)
    gain_table = gh_ref[...] * jnp.where(lane < MLA_QK, cos_ref[...], sin_ref[...])
    for hd in range(N_HEADS):
        sl = slice(hd * LANES, (hd + 1) * LANES)
        x = jnp.dot(h, w_ref[:, sl], preferred_element_type=F32)
        ms = jnp.sum(x * x * real, axis=-1, keepdims=True) / MLA_QK
        q_ref[:, sl] = (x * lax.rsqrt(ms + EPS) * gain_table).astype(q_ref.dtype)


def _mla_k_kernel(ckv_ref, gkv_ref, w_ref, wvt_ref, kr_ref, gh_ref, real_ref, cos_ref, sin_ref,
                  k_ref, vt_ref):
    h = _rms_rows(ckv_ref[...], gkv_ref[...]).astype(BF16)
    cosf = cos_ref[...]
    sinf = sin_ref[...]
    g = gh_ref[...]
    real = real_ref[...]
    k_rope = kr_ref[...]
    rope_sq = jnp.sum(k_rope * k_rope * real, axis=-1, keepdims=True)
    k_rot = k_rope * g
    lane = lax.broadcasted_iota(jnp.int32, k_rot.shape, 1)
    partner = jnp.where(lane < ROT_LO, pltpu.roll(k_rot, LANES - ROPE_HALF, 1),
                        pltpu.roll(k_rot, ROPE_HALF, 1))
    k_rot = k_rot * cosf + partner * sinf
    for hd in range(N_HEADS):
        sl = slice(hd * LANES, (hd + 1) * LANES)
        x = jnp.dot(h, w_ref[:, sl], preferred_element_type=F32)
        ms = (jnp.sum(x * x, axis=-1, keepdims=True) + rope_sq) / MLA_QK
        k_ref[:, sl] = ((x * g + k_rot) * lax.rsqrt(ms + EPS)).astype(k_ref.dtype)
    vt = lax.dot_general(wvt_ref[...], h, (((1,), (1,)), ((), ())), preferred_element_type=F32)
    tk = vt_ref.shape[-1]
    for c in range(vt_ref.shape[0]):
        vt_ref[c] = vt[:, c * tk:(c + 1) * tk].astype(vt_ref.dtype)


def _key_head_gain(g):
    g = g.astype(F32)
    return jnp.concatenate([g, g[MLA_NOPE:]]).reshape(1, LANES)


def _query_head_gain(g):
    g = g.astype(F32)
    return (jnp.concatenate([g, g[ROT_LO:], g[MLA_NOPE:ROT_LO]]) * SOFTMAX_C).reshape(1, LANES)


def _mla_prep(c_q, c_kv, k_rope, cosf, sinf, q_norm, w_q_up, kv_norm, w_kv_up, g_qn, g_kn,
              b, s, tm, tk):
    t = c_q.shape[0]
    wq = w_q_up.reshape(MLA_Q_RANK, N_HEADS, MLA_QK)
    wq = jnp.concatenate([wq, wq[:, :, ROT_LO:], wq[:, :, MLA_NOPE:ROT_LO]], axis=2)
    wq = wq.reshape(MLA_Q_RANK, N_HEADS * LANES).astype(BF16)
    real = (jnp.arange(LANES) < MLA_QK).astype(F32).reshape(1, LANES)
    wkv = w_kv_up.reshape(MLA_KV_RANK, N_HEADS, MLA_NOPE + HEAD_DIM)
    wk = jnp.pad(wkv[:, :, :MLA_NOPE], ((0, 0), (0, 0), (0, LANES - MLA_NOPE)))
    wk = wk.reshape(MLA_KV_RANK, N_HEADS * LANES).astype(BF16)
    wvt = wkv[:, :, MLA_NOPE:].reshape(MLA_KV_RANK, MIX_W).T.astype(BF16)

    row = lambda i: (i, 0)
    fixed = lambda i: (0, 0)
    q = pl.pallas_call(
        _mla_q_kernel,
        grid=(t // tm,),
        in_specs=[
            pl.BlockSpec((tm, MLA_Q_RANK), row),
            pl.BlockSpec((1, MLA_Q_RANK), fixed),
            pl.BlockSpec((MLA_Q_RANK, N_HEADS * LANES), fixed),
            pl.BlockSpec((1, LANES), fixed),
            pl.BlockSpec((1, LANES), fixed),
            pl.BlockSpec((tm, LANES), row),
            pl.BlockSpec((tm, LANES), row),
        ],
        out_specs=pl.BlockSpec((tm, N_HEADS * LANES), row),
        out_shape=jax.ShapeDtypeStruct((t, N_HEADS * LANES), BF16),
        compiler_params=_cparams("parallel"),
        name="mla_q_prep",
    )(c_q, q_norm.reshape(1, -1), wq, _query_head_gain(g_qn), real, cosf, sinf)
    per_seq = s // tm
    k, vt = pl.pallas_call(
        _mla_k_kernel,
        grid=(t // tm,),
        in_specs=[
            pl.BlockSpec((tm, MLA_KV_RANK), row),
            pl.BlockSpec((1, MLA_KV_RANK), fixed),
            pl.BlockSpec((MLA_KV_RANK, N_HEADS * LANES), fixed),
            pl.BlockSpec((MIX_W, MLA_KV_RANK), fixed),
            pl.BlockSpec((tm, LANES), row),
            pl.BlockSpec((1, LANES), fixed),
            pl.BlockSpec((1, LANES), fixed),
            pl.BlockSpec((tm, LANES), row),
            pl.BlockSpec((tm, LANES), row),
        ],
        out_specs=[pl.BlockSpec((tm, N_HEADS * LANES), row),
                   pl.BlockSpec((None, tm // tk, MIX_W, tk),
                                lambda i: (i // per_seq, i % per_seq, 0, 0))],
        out_shape=[jax.ShapeDtypeStruct((t, N_HEADS * LANES), BF16),
                   jax.ShapeDtypeStruct((b, s // tk, MIX_W, tk), BF16)],
        compiler_params=_cparams("parallel"),
        name="mla_kv_prep",
    )(c_kv, kv_norm.reshape(1, -1), wk, wvt, k_rope, _key_head_gain(g_kn), real, cosf, sinf)
    return q, k, vt


def _causal_attn_kernel(q_ref, k_ref, vt_ref, o_ref, sa_ref, sb_ref, *, tq, n_pairs):
    qi = pl.program_id(2)
    key = lax.broadcasted_iota(jnp.int32, (tq, tq), 0)
    qry = lax.broadcasted_iota(jnp.int32, (tq, tq), 1)
    causal = key <= qry
    n_heads = 2 * n_pairs
    qs = [q_ref[:, h * LANES:(h + 1) * LANES] for h in range(n_heads)]

    def score_tiles(kb):
        start = pl.multiple_of(kb * tq, tq)
        return [lax.dot_general(k_ref[pl.ds(start, tq), h * LANES:(h + 1) * LANES], qs[h],
                                (((1,), (1,)), ((), ())), preferred_element_type=F32)
                for h in range(n_heads)]

    ones = jnp.ones((SUM_ROWS, tq), BF16)

    def park(buf, scs, ms, diagonal):
        new_ms, alphas = [], []
        for h in range(n_heads):
            sc = jnp.where(causal, scs[h], -jnp.inf) if diagonal else scs[h]
            m_new = jnp.maximum(ms[h], jnp.max(sc, axis=0, keepdims=True))
            alphas.append(jnp.exp2(ms[h] - m_new))
            new_ms.append(m_new)
            buf[h] = sc
        return tuple(new_ms), tuple(alphas)

    def consume(buf, kb, ms, alphas, accs):
        new_accs = []
        for h in range(n_heads):
            p = jnp.exp2(buf[h] - ms[h]).astype(BF16)
            vt = jnp.concatenate([vt_ref[kb, h * HEAD_DIM:(h + 1) * HEAD_DIM, :], ones], axis=0)
            new_accs.append(alphas[h] * accs[h] + jnp.dot(vt, p, preferred_element_type=F32))
        return tuple(new_accs)

    row = lambda v: (jnp.full((1, tq), v, F32),) * n_heads
    ms, alphas = park(sb_ref, score_tiles(qi), row(-jnp.inf), True)
    accs = (jnp.zeros((HEAD_DIM + SUM_ROWS, tq), F32),) * n_heads

    def double(t, carry):
        ms, alphas, accs = carry
        j = 2 * t
        scs_a = score_tiles(j)
        scs_b = score_tiles(j + 1)
        ms_a, alphas_a = park(sa_ref, scs_a, ms, False)
        accs = consume(sb_ref, jnp.where(j == 0, qi, j - 1), ms, alphas, accs)
        accs = consume(sa_ref, j, ms_a, alphas_a, accs)
        ms, alphas = park(sb_ref, scs_b, ms_a, False)
        return ms, alphas, accs

    ms, alphas, accs = lax.fori_loop(0, qi // 2, double, (ms, alphas, accs))
    before_last = jnp.where(qi <= 1, qi, qi - 2)

    def odd_tail(ms, alphas, accs):
        ms_a, alphas_a = park(sa_ref, score_tiles(qi - 1), ms, False)
        accs = consume(sb_ref, before_last, ms, alphas, accs)
        return consume(sa_ref, qi - 1, ms_a, alphas_a, accs)

    def even_tail(ms, alphas, accs):
        return consume(sb_ref, jnp.where(qi == 0, qi, qi - 1), ms, alphas, accs)

    accs = lax.cond(qi % 2 == 1, odd_tail, even_tail, ms, alphas, accs)
    outs = [acc[:HEAD_DIM] / acc[HEAD_DIM:HEAD_DIM + 1] for acc in accs]
    for p in range(n_pairs):
        pair = jnp.concatenate([outs[2 * p], outs[2 * p + 1]], axis=0)
        o_ref[:, p * LANES:(p + 1) * LANES] = pair.T.astype(o_ref.dtype)


def _causal_attention(q, k, vt, b, s, tq, n_pairs):
    kern = functools.partial(_causal_attn_kernel, tq=tq, n_pairs=n_pairs)
    wq = 2 * n_pairs * LANES
    wv = n_pairs * LANES
    return pl.pallas_call(
        kern,
        grid=(b, N_PAIRS // n_pairs, s // tq),
        in_specs=[
            pl.BlockSpec((None, tq, wq), lambda bi, g, qi: (bi, qi, g)),
            pl.BlockSpec((None, s, wq), lambda bi, g, qi: (bi, 0, g)),
            pl.BlockSpec((None, s // tq, wv, tq), lambda bi, g, qi: (bi, 0, g, 0)),
        ],
        out_specs=pl.BlockSpec((None, tq, wv), lambda bi, g, qi: (bi, qi, g)),
        out_shape=jax.ShapeDtypeStruct((b, s, MIX_W), BF16),
        scratch_shapes=[pltpu.VMEM((2 * n_pairs, tq, tq), F32)] * 2,
        compiler_params=_cparams("parallel", "parallel", "arbitrary"),
        name="causal_attention",
    )(q, k, vt)


def _mem_kv_kernel(kv_ref, g_ref, k_ref, v_ref):
    k = kv_ref[:, :MEM_Q]
    v = kv_ref[:, MEM_Q:]
    g = g_ref[...]
    lane = lax.broadcasted_iota(jnp.int32, k.shape, 1)
    for h in range(N_MEM_HEADS):
        in_head = (lane >= h * HEAD_DIM) & (lane < (h + 1) * HEAD_DIM)
        ms = jnp.sum(jnp.where(in_head, k * k, 0.0), axis=-1, keepdims=True) / HEAD_DIM
        k_ref[h] = jnp.where(in_head, k * lax.rsqrt(ms + EPS) * g, 0.0).astype(k_ref.dtype)
        v_ref[h] = jnp.where(in_head, v, 0.0).astype(v_ref.dtype)


def _mem_out_kernel(mix_ref, qm_ref, k_ref, v_ref, g_ref, w_ref, x_ref, o_ref):
    tm = qm_ref.shape[0]
    rows = [slice(r, r + MEM_ROWS) for r in range(0, tm, MEM_ROWS)]
    outs = [jnp.dot(mix_ref[r, :], w_ref[:MIX_W, :], preferred_element_type=F32) for r in rows]
    lane = lax.broadcasted_iota(jnp.int32, (MEM_ROWS, MEM_Q), 1)
    qns = []
    for r in rows:
        q = qm_ref[r, :]
        inv = jnp.zeros_like(q)
        for h in range(N_MEM_HEADS):
            in_head = (lane >= h * HEAD_DIM) & (lane < (h + 1) * HEAD_DIM)
            ms = jnp.sum(jnp.where(in_head, q * q, 0.0), axis=-1, keepdims=True) / HEAD_DIM
            inv = jnp.where(in_head, lax.rsqrt(ms + EPS), inv)
        qns.append((q * inv * g_ref[...]).astype(BF16))
    scs = [[lax.dot_general(qn, k_ref[h], (((1,), (1,)), ((), ())), preferred_element_type=F32)
            for h in range(N_MEM_HEADS)] for qn in qns]
    ps = []
    for chunk in scs:
        ps.append([])
        for sc in chunk:
            sc = sc * (HEAD_DIM ** -0.5)
            e = jnp.exp(sc - jnp.max(sc, axis=-1, keepdims=True))
            ps[-1].append((e / jnp.sum(e, axis=-1, keepdims=True)).astype(BF16))
    for r, out, p in zip(rows, outs, ps):
        mem_o = jnp.dot(p[0], v_ref[0], preferred_element_type=F32)
        for h in range(1, N_MEM_HEADS):
            mem_o = mem_o + jnp.dot(p[h], v_ref[h], preferred_element_type=F32)
        out = out + jnp.dot(mem_o.astype(BF16), w_ref[MIX_W:, :], preferred_element_type=F32)
        o_ref[r, :] = x_ref[r, :] + out


def _mem_kv_prep(mem2d, ln_g, w_kv, g_kn, b, n_mem):
    w = w_kv.reshape(D_MODEL, N_MEM_HEADS, 2, HEAD_DIM).transpose(0, 2, 1, 3)
    w = w.reshape(D_MODEL, 2 * MEM_Q).astype(BF16)
    (kv,) = _rms_matmul(mem2d, ln_g, w, [(0, 2 * MEM_Q)], [F32], tm=n_mem, name="mem_kv_proj")
    shape = jax.ShapeDtypeStruct((b, N_MEM_HEADS, n_mem, MEM_Q), BF16)
    spec = pl.BlockSpec((None, N_MEM_HEADS, n_mem, MEM_Q), lambda i: (i, 0, 0, 0))
    return pl.pallas_call(
        _mem_kv_kernel,
        grid=(b,),
        in_specs=[pl.BlockSpec((n_mem, 2 * MEM_Q), lambda i: (i, 0)),
                  pl.BlockSpec((1, MEM_Q), lambda i: (0, 0))],
        out_specs=[spec, spec],
        out_shape=[shape, shape],
        compiler_params=_cparams("parallel"),
        name="mem_kv_prep",
    )(kv, jnp.tile(g_kn.astype(F32), N_MEM_HEADS).reshape(1, MEM_Q))


def _mem_out_proj(mix, q_mem, k_mem, v_mem, g_qn, w_out, x, s, tm):
    t = x.shape[0]
    n_mem = k_mem.shape[2]
    per_seq = s // tm
    row = lambda i: (i, 0)
    mem_spec = pl.BlockSpec((None, N_MEM_HEADS, n_mem, MEM_Q), lambda i: (i // per_seq, 0, 0, 0))
    return pl.pallas_call(
        _mem_out_kernel,
        grid=(t // tm,),
        in_specs=[
            pl.BlockSpec((tm, MIX_W), row),
            pl.BlockSpec((tm, MEM_Q), row),
            mem_spec,
            mem_spec,
            pl.BlockSpec((1, MEM_Q), lambda i: (0, 0)),
            pl.BlockSpec((D_MODEL, D_MODEL), lambda i: (0, 0)),
            pl.BlockSpec((tm, D_MODEL), row),
        ],
        out_specs=pl.BlockSpec((tm, D_MODEL), row),
        out_shape=jax.ShapeDtypeStruct((t, D_MODEL), F32),
        compiler_params=_cparams("parallel"),
        name="mem_attn_out_proj",
    )(mix, q_mem, k_mem, v_mem, jnp.tile(g_qn.astype(F32), N_MEM_HEADS).reshape(1, MEM_Q),
      w_out, x)


def _ffn_kernel(x_ref, g_ref, wgu_ref, wd_ref, o_ref, *, tf):
    x = x_ref[...]
    h = _rms_rows(x, g_ref[...]).astype(BF16)
    acc = x
    for c in range(0, D_FF, tf):
        gate = jnp.dot(h, wgu_ref[:, c:c + tf], preferred_element_type=F32)
        up = jnp.dot(h, wgu_ref[:, D_FF + c:D_FF + c + tf], preferred_element_type=F32)
        act = gate * (1.0 / (1.0 + jnp.exp(-gate))) * up
        acc = acc + jnp.dot(act.astype(BF16), wd_ref[c:c + tf, :], preferred_element_type=F32)
    o_ref[...] = acc


def _ffn(x, g, w_gu, w_down, tm, tf):
    t = x.shape[0]
    once = pl.Buffered(1)
    return pl.pallas_call(
        functools.partial(_ffn_kernel, tf=tf),
        grid=(t // tm,),
        in_specs=[
            pl.BlockSpec((tm, D_MODEL), lambda i: (i, 0)),
            pl.BlockSpec((1, D_MODEL), lambda i: (0, 0)),
            pl.BlockSpec((D_MODEL, 2 * D_FF), lambda i: (0, 0), pipeline_mode=once),
            pl.BlockSpec((D_FF, D_MODEL), lambda i: (0, 0), pipeline_mode=once),
        ],
        out_specs=pl.BlockSpec((tm, D_MODEL), lambda i: (i, 0)),
        out_shape=jax.ShapeDtypeStruct((t, D_MODEL), F32),
        compiler_params=_cparams("parallel"),
        name="swiglu_ffn",
    )(x, g.reshape(1, D_MODEL), w_gu, w_down)


def kernel(x, mem, positions, ln_attn, w_out, ln_mem, w_mem_kv, g_qn_mem, g_kn_mem, ln_ffn,
           w_ffn_gu, w_ffn_down, sb_w_in, mla_w_in, mla_q_norm, mla_w_q_up, mla_kv_norm,
           mla_w_kv_up, mla_g_qn, mla_g_kn):
    b, s, d = x.shape
    n_mem = mem.shape[1]
    t = b * s
    x2 = x.reshape(t, d)
    mem2 = mem.reshape(b * n_mem, d)

    qk, q_mem, vt = _sb_in_proj(x2, ln_attn[0], sb_w_in[0], b, s, tm=512, tk=256)
    mix = _sb_attention(qk.reshape(b, s, 2 * MIX_W), vt, b, s, tq=256, n_pairs=3).reshape(t, MIX_W)
    k_mem, v_mem = _mem_kv_prep(mem2, ln_mem[0], w_mem_kv[0], g_kn_mem[0], b, n_mem)
    x2 = _mem_out_proj(mix, q_mem, k_mem, v_mem, g_qn_mem[0], w_out[0].astype(BF16), x2, s, tm=1024)
    x2 = _ffn(x2, ln_ffn[0], w_ffn_gu[0].astype(BF16), w_ffn_down[0].astype(BF16), tm=1024, tf=256)

    w_in = mla_w_in[0]
    o_kr = MLA_Q_RANK + MLA_KV_RANK
    o_qm = o_kr + MLA_ROPE
    w_kr = jnp.concatenate([jnp.zeros((D_MODEL, MLA_NOPE), F32), w_in[:, o_kr:o_qm],
                            w_in[:, o_kr:o_qm]], axis=1)
    w_in = jnp.concatenate([w_in[:, :o_kr], w_in[:, o_qm:], w_kr], axis=1).astype(BF16)
    c_q, c_kv, q_mem, k_rope = _rms_matmul(
        x2, ln_attn[1], w_in,
        [(0, MLA_Q_RANK), (MLA_Q_RANK, MLA_KV_RANK), (o_kr, MEM_Q), (o_kr + MEM_Q, LANES)],
        [F32, F32, F32, F32], tm=512, name="mla_in_proj")
    cosf, sinf = _rope_tables(positions, tm=512)
    q, k, vt = _mla_prep(c_q, c_kv, k_rope, cosf, sinf, mla_q_norm[0], mla_w_q_up[0],
                         mla_kv_norm[0], mla_w_kv_up[0], mla_g_qn[0], mla_g_kn[0], b, s,
                         tm=512, tk=256)
    mix = _causal_attention(q.reshape(b, s, -1), k.reshape(b, s, -1), vt,
                            b, s, tq=256, n_pairs=2).reshape(t, MIX_W)
    k_mem, v_mem = _mem_kv_prep(mem2, ln_mem[1], w_mem_kv[1], g_kn_mem[1], b, n_mem)
    x2 = _mem_out_proj(mix, q_mem, k_mem, v_mem, g_qn_mem[1], w_out[1].astype(BF16), x2, s, tm=1024)
    x2 = _ffn(x2, ln_ffn[1], w_ffn_gu[1].astype(BF16), w_ffn_down[1].astype(BF16), tm=1024, tf=256)
    return x2.reshape(b, s, d)
```
